```python
import math
import jax, jax.numpy as jnp
from jax import lax
import numpy as np

D_MODEL = 1024
BATCH = 8
SEQ = 2048
DEPTH = 1
DEC_BATCH = 128
DEC_SEQ = 8
PAST_LEN = 16384
PAGE_SIZE = 128

DN_HEADS = 8
DN_DK = 128
DN_DV = 128
DN_CONV = 4
DN_CHUNK = 64
QK_W = DN_HEADS * DN_DK
V_W = DN_HEADS * DN_DV
CONV_CH = 2 * QK_W + V_W
S5_CH = 512
S5_GROUP = 16
S5_GROUPS = S5_CH // S5_GROUP
S5_STATE = 64
MOE_GROUPS = 4
MOE_PER_GROUP = 4
MOE_EXPERTS = MOE_GROUPS * MOE_PER_GROUP
MOE_TOPK = 2
MOE_HIDDEN = 256
PLE_DIM = 256
_O_Z = CONV_CH
_O_B = _O_Z + V_W
_O_A = _O_B + DN_HEADS
_O_U = _O_A + DN_HEADS
_O_GA = _O_U + S5_CH
_O_GB = _O_GA + D_MODEL
IN_W = _O_GB + D_MODEL
IN_SPLITS = (QK_W, 2 * QK_W, CONV_CH, _O_B, _O_A, _O_U, _O_GA, _O_GB)
ALPHA = (2 * DEPTH) ** 0.25
BETA = (8 * DEPTH) ** -0.25
LN_EPS = 1e-5
NORM_EPS = 1e-6

kernel_name = "hybrid_deltanet_s5_hmoe_step"

F32 = jnp.float32


def _layer_norm(x, g, b):
    xf = x.astype(F32)
    mu = jnp.mean(xf, -1, keepdims=True)
    var = jnp.mean(jnp.square(xf - mu), -1, keepdims=True)
    return ((xf - mu) * lax.rsqrt(var + LN_EPS) * g + b).astype(x.dtype)


def _l2norm(x):
    return x * lax.rsqrt(jnp.sum(x * x, -1, keepdims=True) + NORM_EPS)


def _causal_conv(x, buf, w):
    t_len = x.shape[1]
    xp = jnp.concatenate([buf.astype(x.dtype), x], axis=1)
    y = sum(xp[:, j:j + t_len] * w[j] for j in range(DN_CONV))
    return jax.nn.silu(y), xp[:, t_len:]


def _gated_delta_rule(q, k, v, g, beta, s0):
    bsz, t_len = q.shape[:2]
    c = min(DN_CHUNK, t_len)
    n_chunks = -(-t_len // c)
    pad = n_chunks * c - t_len

    def blocks(a):
        a = jnp.pad(a, [(0, 0), (0, pad)] + [(0, 0)] * (a.ndim - 2))
        a = a.reshape((bsz, n_chunks, c) + a.shape[2:])
        return jnp.moveaxis(a, 3, 1)

    q, k, v, g, beta = map(blocks, (q * DN_DK ** -0.5, k, v, g, beta))
    gc = jnp.cumsum(g, axis=-1)
    incl = jnp.tril(jnp.ones((c, c), dtype=bool))
    strict = jnp.tril(jnp.ones((c, c), dtype=bool), -1)
    decay = jnp.exp(jnp.where(incl, gc[..., :, None] - gc[..., None, :], -jnp.inf))
    kb = k * beta[..., None]
    a_mat = jnp.where(strict, jnp.einsum('bhncd,bhnsd->bhncs', kb, k) * decay, 0.0)
    eye = jnp.eye(c, dtype=a_mat.dtype)
    t_mat = lax.linalg.triangular_solve(a_mat + eye, jnp.broadcast_to(eye, a_mat.shape),
                                        left_side=True, lower=True, unit_diagonal=True)
    u = jnp.einsum('bhncs,bhnse->bhnce', t_mat, v * beta[..., None])
    w = jnp.einsum('bhncs,bhnsd->bhncd', t_mat, kb * jnp.exp(gc)[..., None])
    qk = jnp.einsum('bhncd,bhnsd->bhncs', q, k) * decay
    q_dec = q * jnp.exp(gc)[..., None]
    k_dec = k * jnp.exp(gc[..., -1:] - gc)[..., None]
    g_tot = jnp.exp(gc[..., -1])

    def step(s, xs):
        u_i, w_i, qk_i, qd_i, kd_i, gt_i = xs
        v_new = u_i - jnp.einsum('bhcd,bhde->bhce', w_i, s)
        o_i = jnp.einsum('bhcd,bhde->bhce', qd_i, s) + jnp.einsum('bhcs,bhse->bhce', qk_i, v_new)
        s = s * gt_i[..., None, None] + jnp.einsum('bhcd,bhce->bhde', kd_i, v_new)
        return s, o_i

    xs = tuple(jnp.moveaxis(t, 2, 0) for t in (u, w, qk, q_dec, k_dec, g_tot))
    s_fin, o = lax.scan(step, s0, xs)
    o = jnp.moveaxis(o, 0, 2).reshape(bsz, DN_HEADS, n_chunks * c, DN_DV)[:, :, :t_len]
    return jnp.moveaxis(o, 1, 2), s_fin


def _s5(u, h0_re, h0_im, lam_re, lam_im, log_dt, b_re, b_im, c_re, c_im, d):
    bsz, t_len, _ = u.shape
    u = u.astype(F32)
    ug = u.reshape(bsz, t_len, S5_GROUPS, S5_GROUP)
    lam_re = lam_re.astype(F32)
    lam_im = lam_im.astype(F32)
    dt = jnp.exp(log_dt.astype(F32))[:, None]
    mag = jnp.exp(lam_re * dt)
    ang = lam_im * dt
    lb_re, lb_im = mag * jnp.cos(ang), mag * jnp.sin(ang)
    den = lam_re * lam_re + lam_im * lam_im
    nr, ni = lb_re - 1.0, lb_im
    f_re = (nr * lam_re + ni * lam_im) / den
    f_im = (ni * lam_re - nr * lam_im) / den
    b_re = b_re.astype(F32)
    b_im = b_im.astype(F32)
    bb_re = f_re[..., None] * b_re - f_im[..., None] * b_im
    bb_im = f_re[..., None] * b_im + f_im[..., None] * b_re
    bu_re = jnp.einsum('btgc,gpc->btgp', ug, bb_re)
    bu_im = jnp.einsum('btgc,gpc->btgp', ug, bb_im)
    h0_re = h0_re.astype(F32)
    h0_im = h0_im.astype(F32)
    bu_re = bu_re.at[:, 0].add(lb_re * h0_re - lb_im * h0_im)
    bu_im = bu_im.at[:, 0].add(lb_re * h0_im + lb_im * h0_re)
    a_re = jnp.broadcast_to(lb_re, bu_re.shape)
    a_im = jnp.broadcast_to(lb_im, bu_im.shape)

    def combine(e1, e2):
        a1r, a1i, b1r, b1i = e1
        a2r, a2i, b2r, b2i = e2
        return (a2r * a1r - a2i * a1i, a2r * a1i + a2i * a1r,
                a2r * b1r - a2i * b1i + b2r, a2r * b1i + a2i * b1r + b2i)

    _, _, h_re, h_im = lax.associative_scan(combine, (a_re, a_im, bu_re, bu_im), axis=1)
    y = (jnp.einsum('btgp,gcp->btgc', h_re, c_re.astype(F32))
         - jnp.einsum('btgp,gcp->btgc', h_im, c_im.astype(F32)))
    y = y.reshape(bsz, t_len, S5_CH) + d.astype(F32) * u
    return y, h_re[:, -1], h_im[:, -1]


def _hier_moe(x, w_rg, b_rg, w_re, b_re, w_gate, w_up, w_down):
    pg = jax.nn.softmax((x @ w_rg).astype(F32) + b_rg, axis=-1)
    g_w, g_idx = lax.top_k(pg, 1)
    le = ((x @ w_re).astype(F32) + b_re).reshape(x.shape[:-1] + (MOE_GROUPS, MOE_PER_GROUP))
    le_sel = jnp.einsum('btgp,btg->btp', le, jax.nn.one_hot(g_idx[..., 0], MOE_GROUPS, dtype=F32))
    e_val, e_loc = lax.top_k(le_sel, MOE_TOPK)
    e_w = jax.nn.softmax(e_val, axis=-1) * g_w
    e_idx = g_idx * MOE_PER_GROUP + e_loc
    comb = jnp.einsum('btk,btke->bte', e_w, jax.nn.one_hot(e_idx, MOE_EXPERTS, dtype=F32))
    h = jax.nn.silu(jnp.einsum('btd,edf->btef', x, w_gate)) * jnp.einsum('btd,edf->btef', x, w_up)
    return jnp.einsum('btef,efd->btd', h * comb[..., None].astype(h.dtype), w_down)


def _layer(x, p, conv_buf, s_delta, h_re, h_im,
           w_in, conv_w, dn_a_log, dn_dt_bias, dn_norm_w, w_dn_out,
           s5_lam_re, s5_lam_im, s5_log_dt, s5_b_re, s5_b_im, s5_c_re, s5_c_im, s5_d, w_glu,
           w_out, ln1_g, ln1_b, w_rg, b_rg, w_re, b_re, w_gate, w_up, w_down, ln2_g, ln2_b,
           w_ple, w_ple_gate, ln3_g, ln3_b):
    bsz, t_len, _ = x.shape
    q, k, v, z, b, a, u, g_a, g_b = jnp.split(x @ w_in, IN_SPLITS, axis=-1)
    qkv, conv_new = _causal_conv(jnp.concatenate([q, k, v], -1), conv_buf, conv_w)
    q, k, v = jnp.split(qkv.astype(F32), (QK_W, 2 * QK_W), axis=-1)
    q = _l2norm(q.reshape(bsz, t_len, DN_HEADS, DN_DK))
    k = _l2norm(k.reshape(bsz, t_len, DN_HEADS, DN_DK))
    v = v.reshape(bsz, t_len, DN_HEADS, DN_DV)
    beta = jax.nn.sigmoid(b.astype(F32))
    g = -jnp.exp(dn_a_log.astype(F32)) * jax.nn.softplus(a.astype(F32) + dn_dt_bias.astype(F32))
    o, s_new = _gated_delta_rule(q, k, v, g, beta, s_delta.astype(F32))
    o = o * lax.rsqrt(jnp.mean(o * o, -1, keepdims=True) + NORM_EPS) * dn_norm_w.astype(F32)
    o = o * jax.nn.silu(z.astype(F32).reshape(bsz, t_len, DN_HEADS, DN_DV))
    branch_a = o.reshape(bsz, t_len, V_W).astype(x.dtype) @ w_dn_out
    y_b, hr_new, hi_new = _s5(u, h_re, h_im, s5_lam_re, s5_lam_im, s5_log_dt,
                              s5_b_re, s5_b_im, s5_c_re, s5_c_im, s5_d)
    glu = jax.nn.gelu(y_b).astype(x.dtype) @ w_glu
    branch_b = glu[..., :D_MODEL] * jax.nn.sigmoid(glu[..., D_MODEL:])
    mix = (jax.nn.sigmoid(g_a) * branch_a + jax.nn.sigmoid(g_b) * branch_b) @ w_out
    x = _layer_norm(ALPHA * x + mix, ln1_g, ln1_b)
    x = _layer_norm(ALPHA * x + _hier_moe(x, w_rg, b_rg, w_re, b_re, w_gate, w_up, w_down), ln2_g, ln2_b)
    ple = jax.nn.sigmoid(x @ w_ple_gate) * (p @ w_ple)
    x = _layer_norm(ALPHA * x + ple, ln3_g, ln3_b)
    return x, s_new, conv_new, hr_new, hi_new


def setup_inputs(seed: int = 0) -> dict:
    key = jax.random.key(seed)
    ks = iter(jax.random.split(key, 48))
    L = DEPTH

    def nrm(shape, scale):
        return jax.random.normal(next(ks), shape, F32) * scale

    def unif(shape, lo, hi):
        return jax.random.uniform(next(ks), shape, F32, lo, hi)

    dn_dt = jnp.exp(unif((L, DN_HEADS), math.log(1e-3), math.log(1e-1)))
    return {
        "x_prompt": nrm((BATCH, SEQ, D_MODEL), 1.0),
        "x_sample": nrm((DEC_BATCH, DEC_SEQ, D_MODEL), 1.0),
        "state_delta": nrm((L, DEC_BATCH, DN_HEADS, DN_DK, DN_DV), 0.05),
        "state_conv": nrm((L, DEC_BATCH, DN_CONV - 1, CONV_CH), 1.0),
        "state_ssm_re": nrm((L, DEC_BATCH, S5_GROUPS, S5_STATE), 0.1),
        "state_ssm_im": nrm((L, DEC_BATCH, S5_GROUPS, S5_STATE), 0.1),
        "p_prompt": nrm((L, BATCH, SEQ, PLE_DIM), 1.0),
        "p_sample": nrm((L, DEC_BATCH, DEC_SEQ, PLE_DIM), 1.0),
        "w_in": nrm((L, D_MODEL, IN_W), D_MODEL ** -0.5),
        "conv_w": nrm((L, DN_CONV, CONV_CH), DN_CONV ** -0.5),
        "dn_a_log": jnp.log(unif((L, DN_HEADS), 1.0, 16.0)),
        "dn_dt_bias": jnp.log(jnp.expm1(dn_dt)),
        "dn_norm_w": 1.0 + nrm((L, DN_DV), 0.01),
        "w_dn_out": nrm((L, V_W, D_MODEL), V_W ** -0.5),
        "s5_lam_re": -0.5 + nrm((L, S5_GROUPS, S5_STATE), 0.01),
        "s5_lam_im": jnp.pi * jnp.arange(S5_STATE, dtype=F32) + nrm((L, S5_GROUPS, S5_STATE), 0.01),
        "s5_log_dt": unif((L, S5_GROUPS), math.log(1e-3), math.log(1e-1)),
        "s5_b_re": nrm((L, S5_GROUPS, S5_STATE, S5_GROUP), (2 * S5_GROUP) ** -0.5),
        "s5_b_im": nrm((L, S5_GROUPS, S5_STATE, S5_GROUP), (2 * S5_GROUP) ** -0.5),
        "s5_c_re": nrm((L, S5_GROUPS, S5_GROUP, S5_STATE), S5_STATE ** -0.5),
        "s5_c_im": nrm((L, S5_GROUPS, S5_GROUP, S5_STATE), S5_STATE ** -0.5),
        "s5_d": nrm((L, S5_CH), 1.0),
        "w_glu": nrm((L, S5_CH, 2 * D_MODEL), S5_CH ** -0.5),
        "w_out": nrm((L, D_MODEL, D_MODEL), BETA * D_MODEL ** -0.5),
        "ln1_g": 1.0 + nrm((L, D_MODEL), 0.01),
        "ln1_b": nrm((L, D_MODEL), 0.01),
        "w_rg": nrm((L, D_MODEL, MOE_GROUPS), D_MODEL ** -0.5),
        "b_rg": nrm((L, MOE_GROUPS), 0.01),
        "w_re": nrm((L, D_MODEL, MOE_EXPERTS), D_MODEL ** -0.5),
        "b_re": nrm((L, MOE_EXPERTS), 0.01),
        "w_gate": nrm((L, MOE_EXPERTS, D_MODEL, MOE_HIDDEN), D_MODEL ** -0.5),
        "w_up": nrm((L, MOE_EXPERTS, D_MODEL, MOE_HIDDEN), D_MODEL ** -0.5),
        "w_down": nrm((L, MOE_EXPERTS, MOE_HIDDEN, D_MODEL), BETA * MOE_HIDDEN ** -0.5),
        "ln2_g": 1.0 + nrm((L, D_MODEL), 0.01),
        "ln2_b": nrm((L, D_MODEL), 0.01),
        "w_ple": nrm((L, PLE_DIM, D_MODEL), BETA * PLE_DIM ** -0.5),
        "w_ple_gate": nrm((L, D_MODEL, D_MODEL), D_MODEL ** -0.5),
        "ln3_g": 1.0 + nrm((L, D_MODEL), 0.01),
        "ln3_b": nrm((L, D_MODEL), 0.01),
    }


def reference(x_prompt, x_sample, state_delta, state_conv, state_ssm_re, state_ssm_im,
              p_prompt, p_sample,
              w_in, conv_w, dn_a_log, dn_dt_bias, dn_norm_w, w_dn_out,
              s5_lam_re, s5_lam_im, s5_log_dt, s5_b_re, s5_b_im, s5_c_re, s5_c_im, s5_d, w_glu,
              w_out, ln1_g, ln1_b, w_rg, b_rg, w_re, b_re, w_gate, w_up, w_down, ln2_g, ln2_b,
              w_ple, w_ple_gate, ln3_g, ln3_b):
    weights = (w_in, conv_w, dn_a_log, dn_dt_bias, dn_norm_w, w_dn_out,
               s5_lam_re, s5_lam_im, s5_log_dt, s5_b_re, s5_b_im, s5_c_re, s5_c_im, s5_d, w_glu,
               w_out, ln1_g, ln1_b, w_rg, b_rg, w_re, b_re, w_gate, w_up, w_down, ln2_g, ln2_b,
               w_ple, w_ple_gate, ln3_g, ln3_b)
    y_p, y_s = x_prompt, x_sample
    bp = x_prompt.shape[0]
    dp, cp, rp, ip = [], [], [], []
    ds, cs, rs, is_ = [], [], [], []
    for i in range(DEPTH):
        lw = [wt[i] for wt in weights]
        y_p, s_d, s_c, s_r, s_i = _layer(
            y_p, p_prompt[i],
            jnp.zeros((bp, DN_CONV - 1, CONV_CH), x_prompt.dtype),
            jnp.zeros((bp, DN_HEADS, DN_DK, DN_DV), F32),
            jnp.zeros((bp, S5_GROUPS, S5_STATE), F32),
            jnp.zeros((bp, S5_GROUPS, S5_STATE), F32), *lw)
        dp.append(s_d); cp.append(s_c); rp.append(s_r); ip.append(s_i)
        y_s, s_d, s_c, s_r, s_i = _layer(
            y_s, p_sample[i], state_conv[i], state_delta[i], state_ssm_re[i], state_ssm_im[i], *lw)
        ds.append(s_d); cs.append(s_c); rs.append(s_r); is_.append(s_i)
    return (y_p, y_s,
            jnp.stack(dp), jnp.stack(cp), jnp.stack(rp), jnp.stack(ip),
            jnp.stack(ds), jnp.stack(cs), jnp.stack(rs), jnp.stack(is_))
```

```python
import functools

import jax
import jax.numpy as jnp
from jax import lax
from jax.experimental import pallas as pl
from jax.experimental.pallas import tpu as pltpu

F32 = jnp.float32
BF16 = jnp.bfloat16

D_MODEL = 1024
DN_HEADS = 8
DN_DK = 128
DN_DV = 128
DN_CONV = 4
DN_CHUNK = 64
QK_W = DN_HEADS * DN_DK
V_W = DN_HEADS * DN_DV
CONV_CH = 2 * QK_W + V_W
S5_CH = 512
S5_GROUP = 16
S5_GROUPS = S5_CH // S5_GROUP
S5_STATE = 64
S5_LANES = S5_GROUPS * S5_STATE
MOE_GROUPS = 4
MOE_PER_GROUP = 4
MOE_EXPERTS = MOE_GROUPS * MOE_PER_GROUP
MOE_HIDDEN = 256
PLE_DIM = 256
LN_EPS = 1e-5
NORM_EPS = 1e-6

LANES = 128
SUBLANES = 8
VMEM_LIMIT_BYTES = 56 * 1024 * 1024

ROW_TILE = 256
MOE_TILE = 512
S5_ROWS = 512
DELTA_ROWS = 256
NEG_BIG = -1e30


def _params(*sem):
    return pltpu.CompilerParams(dimension_semantics=sem, vmem_limit_bytes=VMEM_LIMIT_BYTES)


def _mm(a, b):
    return jnp.dot(a.astype(BF16), b.astype(BF16), preferred_element_type=F32)


def _mm_nt(a, b):
    return lax.dot_general(a.astype(BF16), b.astype(BF16), (((1,), (1,)), ((), ())),
                           preferred_element_type=F32)


def _split3(x):
    x1 = x.astype(BF16)
    r1 = x - x1.astype(F32)
    x2 = r1.astype(BF16)
    r2 = r1 - x2.astype(F32)
    return x1, x2, r2.astype(BF16)


def _split2(x):
    x1 = x.astype(BF16)
    return x1, (x - x1.astype(F32)).astype(BF16)


def _silu(x):
    return x * jax.nn.sigmoid(x)


def _layer_norm(x, g, b):
    mu = jnp.mean(x, -1, keepdims=True)
    xc = x - mu
    var = jnp.mean(xc * xc, -1, keepdims=True)
    return xc * lax.rsqrt(var + LN_EPS) * g + b


def _s5_param_kernel(lre_ref, lim_ref, ldt_ref, bre_ref, bim_ref, lbre_ref, lbim_ref, bbre_ref, bbim_ref):
    lam_re = lre_ref[...]
    lam_im = lim_ref[...]
    dt = jnp.exp(ldt_ref[...])
    mag = jnp.exp(lam_re * dt)
    ang = lam_im * dt
    lb_re = mag * jnp.cos(ang)
    lb_im = mag * jnp.sin(ang)
    den = lam_re * lam_re + lam_im * lam_im
    nr = lb_re - 1.0
    ni = lb_im
    f_re = (nr * lam_re + ni * lam_im) / den
    f_im = (ni * lam_re - nr * lam_im) / den
    b_re = bre_ref[...]
    b_im = bim_ref[...]
    lbre_ref[...] = lb_re
    lbim_ref[...] = lb_im
    bbre_ref[...] = f_re * b_re - f_im * b_im
    bbim_ref[...] = f_re * b_im + f_im * b_re


def _s5_params(lam_re, lam_im, log_dt, b_re, b_im):
    row = lambda a: a.reshape(1, S5_LANES)
    ldt = jnp.repeat(log_dt, S5_STATE).reshape(1, S5_LANES)
    bt = lambda b: jnp.transpose(b, (2, 0, 1)).reshape(S5_GROUP, S5_LANES)
    sds = jax.ShapeDtypeStruct
    lb_re, lb_im, bb_re, bb_im = pl.pallas_call(
        _s5_param_kernel,
        out_shape=(sds((1, S5_LANES), F32), sds((1, S5_LANES), F32),
                   sds((S5_GROUP, S5_LANES), F32), sds((S5_GROUP, S5_LANES), F32)),
    )(row(lam_re), row(lam_im), ldt, bt(b_re), bt(b_im))
    eye = jnp.eye(S5_GROUPS, dtype=F32)

    def block_diag(bb):
        bb = bb.reshape(S5_GROUP, S5_GROUPS, S5_STATE)
        return (eye[:, None, :, None] * bb[None]).reshape(S5_CH, S5_LANES)

    w_b = jnp.concatenate([block_diag(bb_re), block_diag(bb_im)], axis=1).astype(BF16)
    return lb_re, lb_im, w_b


def _s5_c_weights(c_re, c_im):
    eye = jnp.eye(S5_GROUPS, dtype=F32)

    def block_diag(c):
        return (jnp.transpose(c, (0, 2, 1))[:, :, None, :] * eye[:, None, :, None]).reshape(S5_LANES, S5_CH)

    return block_diag(c_re).astype(BF16), block_diag(c_im).astype(BF16)


def _inproj_kernel(x_ref, cbuf_ref, wqkv_ref, wz_ref, wba_ref, wu_ref, convw_ref, alog_ref, dtb_ref,
                   q_ref, k_ref, v_ref, gz_ref, bg_ref, u_ref, cnew_ref, ext_ref, *, bb, tt):
    tm = bb * tt
    pad = SUBLANES
    t = pl.program_id(1)
    xb = x_ref[...].astype(BF16)

    @pl.when(t == 0)
    def _():
        ext_ref[:, pad - 3:pad, :] = cbuf_ref[...]

    pre = jnp.dot(xb, wqkv_ref[...], preferred_element_type=F32)
    ext_ref[:, pad:pad + tt, :] = pre.reshape(bb, tt, CONV_CH)

    for h in range(CONV_CH // LANES):
        cs = slice(h * LANES, (h + 1) * LANES)
        acc = ext_ref[:, pad:pad + tt, cs] * convw_ref[3:4, cs]
        for j in range(DN_CONV - 1):
            acc = acc + ext_ref[:, pad - 3 + j:pad - 3 + j + tt, cs] * convw_ref[j:j + 1, cs]
        y = _silu(acc).reshape(tm, LANES)
        if h < 2 * DN_HEADS:
            y = y * lax.rsqrt(jnp.sum(y * y, -1, keepdims=True) + NORM_EPS)
        if h < DN_HEADS:
            q_ref[:, cs] = y * (DN_DK ** -0.5)
        elif h < 2 * DN_HEADS:
            k_ref[:, slice(cs.start - QK_W, cs.stop - QK_W)] = y
        else:
            v_ref[:, slice(cs.start - 2 * QK_W, cs.stop - 2 * QK_W)] = y

    carry = ext_ref[:, pad + tt - 3:pad + tt, :]
    cnew_ref[...] = carry
    ext_ref[:, pad - 3:pad, :] = carry

    z = jnp.dot(xb, wz_ref[...], preferred_element_type=F32)
    gz_ref[...] = _silu(z)
    u_ref[...] = jnp.dot(xb, wu_ref[...], preferred_element_type=F32)

    ba = jnp.dot(xb, wba_ref[...], preferred_element_type=F32)
    lane = lax.broadcasted_iota(jnp.int32, ba.shape, 1)
    beta = jax.nn.sigmoid(ba)
    sp_in = ba + dtb_ref[...]
    softplus = jnp.maximum(sp_in, 0.0) + jnp.log1p(jnp.exp(-jnp.abs(sp_in)))
    g = -jnp.exp(alog_ref[...]) * softplus
    bg = jnp.where(lane < DN_HEADS, beta, jnp.where(lane < 2 * DN_HEADS, g, 0.0))
    bg_ref[...] = bg


def _inproj(x, conv_buf, wqkv, wz, wba, wu, conv_w, alog_row, dtb_row, *, t_len, bb, tt):
    m = x.shape[0]
    bsz = m // t_len
    n_t = t_len // tt
    grid = (bsz // bb, n_t)
    tok = lambda w: pl.BlockSpec((bb * tt, w), lambda b, t: (b * n_t + t, 0))
    full = lambda a: pl.BlockSpec(a.shape, lambda b, t: (0,) * a.ndim)
    sds = jax.ShapeDtypeStruct
    return pl.pallas_call(
        functools.partial(_inproj_kernel, bb=bb, tt=tt),
        grid=grid,
        in_specs=[tok(D_MODEL), pl.BlockSpec((bb, DN_CONV - 1, CONV_CH), lambda b, t: (b, 0, 0)),
                  full(wqkv), full(wz), full(wba), full(wu), full(conv_w), full(alog_row), full(dtb_row)],
        out_specs=[tok(QK_W), tok(QK_W), tok(V_W), tok(V_W), tok(LANES), tok(S5_CH),
                   pl.BlockSpec((bb, DN_CONV - 1, CONV_CH), lambda b, t: (b, 0, 0))],
        out_shape=[sds((m, QK_W), F32), sds((m, QK_W), F32), sds((m, V_W), F32), sds((m, V_W), F32),
                   sds((m, LANES), F32), sds((m, S5_CH), F32), sds((bsz, DN_CONV - 1, CONV_CH), F32)],
        scratch_shapes=[pltpu.VMEM((bb, SUBLANES + tt, CONV_CH), F32)],
        compiler_params=_params("parallel", "arbitrary"),
        name="inproj",
    )(x, conv_buf, wqkv, wz, wba, wu, conv_w, alog_row, dtb_row)


def _delta_intra(q, k, v, bgv, head, chunk):
    rows = DELTA_ROWS
    lane = lax.broadcasted_iota(jnp.int32, (rows, LANES), 1)
    beta = jnp.sum(jnp.where(lane == head, bgv, 0.0), -1, keepdims=True)
    g = jnp.sum(jnp.where(lane == head + DN_HEADS, bgv, 0.0), -1, keepdims=True)

    ri = lax.broadcasted_iota(jnp.int32, (rows, rows), 0)
    ci = lax.broadcasted_iota(jnp.int32, (rows, rows), 1)
    shift = chunk.bit_length() - 1
    same = (ri >> shift) == (ci >> shift)
    incl = same & (ri >= ci)
    strict = same & (ri > ci)

    gb = jnp.broadcast_to(g, (rows, LANES))
    sel = jnp.concatenate([jnp.where(incl, 1.0, 0.0), jnp.where(same, 1.0, 0.0)], axis=0).astype(BF16)
    g1, g2, g3 = _split3(gb)
    cs = (jnp.dot(sel, g1, preferred_element_type=F32) + jnp.dot(sel, g2, preferred_element_type=F32)
          + jnp.dot(sel, g3, preferred_element_type=F32))
    gc = cs[:rows]
    g_last = cs[rows:]
    egc = jnp.exp(gc)

    gc_sq = jnp.concatenate([gc, gc], axis=1)
    diff = gc_sq - gc_sq.T
    decay = jnp.where(incl, jnp.exp(jnp.where(incl, diff, 0.0)), 0.0)

    kb = k * beta
    a_mat = jnp.where(strict, _mm_nt(kb, k) * decay, 0.0)
    x_mat = -a_mat
    p_mat = a_mat
    for _ in range(shift - 1):
        p_mat = _mm(p_mat, p_mat)
        x_mat = x_mat + p_mat + _mm(x_mat, p_mat)
    rhs = jnp.concatenate([v * beta, kb * egc], axis=1)
    uw = rhs + _mm(x_mat, rhs)
    u = uw[:, :DN_DV]
    w = uw[:, DN_DV:]
    qk = jnp.where(incl, _mm_nt(q, k) * decay, 0.0)
    q_dec = q * egc
    k_dec = k * jnp.exp(g_last - gc)
    return u, w, qk, q_dec, k_dec, jnp.exp(g_last)


def _delta_finish(o, gz, nw_ref, og_ref):
    o = o * lax.rsqrt(jnp.mean(o * o, -1, keepdims=True) + NORM_EPS) * nw_ref[...]
    og_ref[...] = (o * gz).astype(BF16)


def _delta_prompt_kernel(q_ref, k_ref, v_ref, gz_ref, bg_ref, s0_ref, nw_ref, og_ref, sout_ref, s_ref):
    rows, c = DELTA_ROWS, DN_CHUNK
    head = pl.program_id(1)
    t = pl.program_id(2)

    @pl.when(t == 0)
    def _():
        s_ref[...] = s0_ref[0, 0]

    u, w, qk, q_dec, k_dec, g_tot = _delta_intra(q_ref[...], k_ref[...], v_ref[...], bg_ref[...], head, c)
    kd_t = k_dec.T
    col = lax.broadcasted_iota(jnp.int32, kd_t.shape, 1)
    s = s_ref[...]
    vn_parts, qs_parts = [], []
    zeros = jnp.zeros((c, DN_DV), F32)
    for j in range(rows // c):
        sl = slice(j * c, (j + 1) * c)
        r = _mm(jnp.concatenate([w[sl], q_dec[sl]], axis=0), s)
        vn = u[sl] - r[:c]
        vn_parts.append(vn)
        qs_parts.append(r[c:])
        vn_pad = jnp.concatenate([vn if i == j else zeros for i in range(rows // c)], axis=0)
        kd_j = jnp.where((col >= j * c) & (col < (j + 1) * c), kd_t, 0.0)
        s = s * g_tot[j * c:j * c + 1, :] + _mm(kd_j, vn_pad)
    s_ref[...] = s
    o = jnp.concatenate(qs_parts, axis=0) + _mm(qk, jnp.concatenate(vn_parts, axis=0))
    _delta_finish(o, gz_ref[...], nw_ref, og_ref)

    @pl.when(t == pl.num_programs(2) - 1)
    def _():
        sout_ref[0, 0] = s


def _delta_sample_kernel(q_ref, k_ref, v_ref, gz_ref, bg_ref, s0_ref, nw_ref, og_ref, sout_ref, *, tt):
    rows, c = DELTA_ROWS, tt
    nseq = rows // c
    head = pl.program_id(1)
    u, w, qk, q_dec, k_dec, g_tot = _delta_intra(q_ref[...], k_ref[...], v_ref[...], bg_ref[...], head, c)
    kd_t = k_dec.T
    col = lax.broadcasted_iota(jnp.int32, kd_t.shape, 1)
    row = lax.broadcasted_iota(jnp.int32, (rows, DN_DV), 0)
    vn_parts, qs_parts = [], []
    for i in range(nseq):
        sl = slice(i * c, (i + 1) * c)
        s = s0_ref[i, 0]
        r = _mm(jnp.concatenate([w[sl], q_dec[sl]], axis=0), s)
        vn_parts.append(u[sl] - r[:c])
        qs_parts.append(r[c:])
    vn_all = jnp.concatenate(vn_parts, axis=0)
    for i in range(nseq):
        kd_i = jnp.where((col >= i * c) & (col < (i + 1) * c), kd_t, 0.0)
        sout_ref[i, 0] = s0_ref[i, 0] * g_tot[i * c:i * c + 1, :] + _mm(kd_i, vn_all)
    o = jnp.concatenate(qs_parts, axis=0) + _mm(qk, vn_all)
    _delta_finish(o, gz_ref[...], nw_ref, og_ref)


def _delta(q, k, v, gz, bg, s0, norm_w, *, t_len):
    m = q.shape[0]
    bsz = m // t_len
    sds = jax.ShapeDtypeStruct
    out_shape = [sds((m, V_W), BF16), sds((bsz, DN_HEADS, DN_DK, DN_DV), F32)]
    if t_len % DELTA_ROWS == 0:
        n_t = t_len // DELTA_ROWS
        grid = (bsz, DN_HEADS, n_t)
        hd = pl.BlockSpec((DELTA_ROWS, LANES), lambda b, h, t: (b * n_t + t, h))
        st = pl.BlockSpec((1, 1, DN_DK, DN_DV), lambda b, h, t: (b, h, 0, 0))
        return pl.pallas_call(
            _delta_prompt_kernel, grid=grid,
            in_specs=[hd, hd, hd, hd, pl.BlockSpec((DELTA_ROWS, LANES), lambda b, h, t: (b * n_t + t, 0)), st,
                      pl.BlockSpec((1, LANES), lambda b, h, t: (0, 0))],
            out_specs=[hd, st], out_shape=out_shape,
            scratch_shapes=[pltpu.VMEM((DN_DK, DN_DV), F32)],
            compiler_params=_params("parallel", "parallel", "arbitrary"),
            name="delta_prompt",
        )(q, k, v, gz, bg, s0, norm_w)
    assert t_len <= DN_CHUNK and DELTA_ROWS % t_len == 0 and t_len % SUBLANES == 0
    nseq = DELTA_ROWS // t_len
    assert bsz % nseq == 0
    grid = (bsz // nseq, DN_HEADS)
    hd = pl.BlockSpec((DELTA_ROWS, LANES), lambda b, h: (b, h))
    st = pl.BlockSpec((nseq, 1, DN_DK, DN_DV), lambda b, h: (b, h, 0, 0))
    return pl.pallas_call(
        functools.partial(_delta_sample_kernel, tt=t_len), grid=grid,
        in_specs=[hd, hd, hd, hd, pl.BlockSpec((DELTA_ROWS, LANES), lambda b, h: (b, 0)), st,
                  pl.BlockSpec((1, LANES), lambda b, h: (0, 0))],
        out_specs=[hd, st], out_shape=out_shape,
        compiler_params=_params("parallel", "parallel"),
        name="delta_sample",
    )(q, k, v, gz, bg, s0, norm_w)


def _s5_kernel(u_ref, h0re_ref, h0im_ref, wb_ref, lbre_ref, lbim_ref, wcre_ref, wcim_ref, d_ref,
               yb_ref, hre_ref, him_ref, hbuf_ref, hst_ref, *, bb, tt):
    rows = bb * tt
    t = pl.program_id(1)
    n_tiles = S5_LANES // LANES
    u2 = u_ref[...].reshape(rows, S5_CH)
    bu = jnp.dot(u2.astype(BF16), wb_ref[...], preferred_element_type=F32)
    for c in range(2 * n_tiles):
        hbuf_ref[c] = bu[:, c * LANES:(c + 1) * LANES]

    @pl.when(t == 0)
    def _():
        hst_ref[:, :, :S5_LANES] = h0re_ref[...].reshape(bb // SUBLANES, SUBLANES, S5_LANES)
        hst_ref[:, :, S5_LANES:] = h0im_ref[...].reshape(bb // SUBLANES, SUBLANES, S5_LANES)

    blk = 4
    for gi in range(bb // SUBLANES):
        for c0 in range(0, n_tiles, blk):
            lanes = [slice((c0 + j) * LANES, (c0 + j + 1) * LANES) for j in range(blk)]
            a_re = [lbre_ref[:, s] for s in lanes]
            a_im = [lbim_ref[:, s] for s in lanes]

            def step(i, carry):
                idx = pl.ds(gi * SUBLANES * tt + i, SUBLANES, stride=tt)
                new = []
                for j in range(blk):
                    h_re, h_im = carry[j]
                    n_re = a_re[j] * h_re - a_im[j] * h_im + hbuf_ref[c0 + j, idx, :]
                    n_im = a_re[j] * h_im + a_im[j] * h_re + hbuf_ref[n_tiles + c0 + j, idx, :]
                    hbuf_ref[c0 + j, idx, :] = n_re
                    hbuf_ref[n_tiles + c0 + j, idx, :] = n_im
                    new.append((n_re, n_im))
                return tuple(new)

            init = tuple((hst_ref[gi, :, s], hst_ref[gi, :, slice(S5_LANES + s.start, S5_LANES + s.stop)])
                         for s in lanes)
            fin = lax.fori_loop(0, tt, step, init)
            for j, s in enumerate(lanes):
                hst_ref[gi, :, s] = fin[j][0]
                hst_ref[gi, :, slice(S5_LANES + s.start, S5_LANES + s.stop)] = fin[j][1]

    h_re_all = jnp.concatenate([hbuf_ref[c] for c in range(n_tiles)], axis=1)
    h_im_all = jnp.concatenate([hbuf_ref[n_tiles + c] for c in range(n_tiles)], axis=1)
    y = (jnp.dot(h_re_all.astype(BF16), wcre_ref[...], preferred_element_type=F32)
         - jnp.dot(h_im_all.astype(BF16), wcim_ref[...], preferred_element_type=F32))
    y = y + d_ref[...] * u2
    yb_ref[...] = jax.nn.gelu(y).reshape(yb_ref.shape).astype(BF16)
    hre_ref[...] = hst_ref[:, :, :S5_LANES].reshape(bb, S5_LANES)
    him_ref[...] = hst_ref[:, :, S5_LANES:].reshape(bb, S5_LANES)


def _s5(u, h0_re, h0_im, w_b, lb_re, lb_im, wc_re, wc_im, d_row, *, t_len, bb, tt):
    m = u.shape[0]
    bsz = m // t_len
    grid = (bsz // bb, t_len // tt)
    if tt == t_len:
        view = (1, m, S5_CH)
        tok = pl.BlockSpec((1, bb * tt, S5_CH), lambda b, t: (0, b, 0))
    else:
        view = (bsz, t_len, S5_CH)
        tok = pl.BlockSpec((bb, tt, S5_CH), lambda b, t: (b, t, 0))
    full = lambda a: pl.BlockSpec(a.shape, lambda b, t: (0,) * a.ndim)
    st = pl.BlockSpec((bb, S5_LANES), lambda b, t: (b, 0))
    sds = jax.ShapeDtypeStruct
    yb, h_re, h_im = pl.pallas_call(
        functools.partial(_s5_kernel, bb=bb, tt=tt), grid=grid,
        in_specs=[tok, st, st, full(w_b), full(lb_re), full(lb_im), full(wc_re), full(wc_im), full(d_row)],
        out_specs=[tok, st, st],
        out_shape=[sds(view, BF16), sds((bsz, S5_LANES), F32), sds((bsz, S5_LANES), F32)],
        scratch_shapes=[pltpu.VMEM((2 * S5_LANES // LANES, bb * tt, LANES), F32),
                        pltpu.VMEM((bb // SUBLANES, SUBLANES, 2 * S5_LANES), F32)],
        compiler_params=_params("parallel", "arbitrary"),
        name="s5",
    )(u.reshape(view), h0_re, h0_im, w_b, lb_re, lb_im, wc_re, wc_im, d_row)
    return yb.reshape(m, S5_CH), h_re, h_im


def _merge_kernel(x_ref, og_ref, yb_ref, wdn_ref, wglu_ref, wgab_ref, wout_ref, g_ref, b_ref, o_ref, *, alpha):
    x = x_ref[...]
    branch_a = jnp.dot(og_ref[...], wdn_ref[...], preferred_element_type=F32)
    glu = jnp.dot(yb_ref[...], wglu_ref[...], preferred_element_type=F32)
    branch_b = glu[:, :D_MODEL] * jax.nn.sigmoid(glu[:, D_MODEL:])
    gab = jnp.dot(x.astype(BF16), wgab_ref[...], preferred_element_type=F32)
    mix_in = jax.nn.sigmoid(gab[:, :D_MODEL]) * branch_a + jax.nn.sigmoid(gab[:, D_MODEL:]) * branch_b
    mix = jnp.dot(mix_in.astype(BF16), wout_ref[...], preferred_element_type=F32)
    o_ref[...] = _layer_norm(alpha * x + mix, g_ref[...], b_ref[...])


def _merge(x, og, yb, wdn, wglu, wgab, wout, ln_g, ln_b, *, alpha):
    m = x.shape[0]
    tm = ROW_TILE
    row = lambda w: pl.BlockSpec((tm, w), lambda i: (i, 0))
    full = lambda a: pl.BlockSpec(a.shape, lambda i: (0,) * a.ndim)
    return pl.pallas_call(
        functools.partial(_merge_kernel, alpha=alpha), grid=(m // tm,),
        in_specs=[row(D_MODEL), row(V_W), row(S5_CH), full(wdn), full(wglu), full(wgab), full(wout),
                  full(ln_g), full(ln_b)],
        out_specs=row(D_MODEL), out_shape=jax.ShapeDtypeStruct((m, D_MODEL), F32),
        compiler_params=_params("parallel"),
        name="merge",
    )(x, og, yb, wdn, wglu, wgab, wout, ln_g, ln_b)


def _route(x, wr_hi_ref, wr_lo_ref, br_ref):
    x_hi, x_lo = _split2(x)
    logits = (jnp.dot(x_hi, wr_hi_ref[...], preferred_element_type=F32)
              + jnp.dot(x_hi, wr_lo_ref[...], preferred_element_type=F32)
              + jnp.dot(x_lo, wr_hi_ref[...], preferred_element_type=F32)) + br_ref[...]
    lane = lax.broadcasted_iota(jnp.int32, logits.shape, 1).astype(F32)
    first = lambda mask: jnp.min(jnp.where(mask, lane, float(LANES)), -1, keepdims=True)
    lg = jnp.where(lane < MOE_GROUPS, logits, NEG_BIG)
    g_max = jnp.max(lg, -1, keepdims=True)
    g_idx = first(lg == g_max)
    g_w = 1.0 / jnp.sum(jnp.where(lane < MOE_GROUPS, jnp.exp(lg - g_max), 0.0), -1, keepdims=True)
    lo = MOE_GROUPS + MOE_PER_GROUP * g_idx
    le = jnp.where((lane >= lo) & (lane < lo + MOE_PER_GROUP), logits, NEG_BIG)
    m1 = jnp.max(le, -1, keepdims=True)
    i1 = first(le == m1)
    le2 = jnp.where(lane == i1, NEG_BIG, le)
    m2 = jnp.max(le2, -1, keepdims=True)
    i2 = first(le2 == m2)
    e2 = jnp.exp(m2 - m1)
    w1 = g_w / (1.0 + e2)
    w2 = g_w * e2 / (1.0 + e2)
    return jnp.where(lane == i1, w1, 0.0) + jnp.where(lane == i2, w2, 0.0)


def _moe_kernel(x_ref, p_ref, wrh_ref, wrl_ref, br_ref, wgu_ref, wd_ref, g2_ref, b2_ref, wpg_ref, wple_ref,
                g3_ref, b3_ref, o_ref, acc_ref, comb_ref, xb_ref, *, alpha):
    e = pl.program_id(1)

    @pl.when(e == 0)
    def _():
        x = x_ref[...]
        comb_ref[...] = _route(x, wrh_ref, wrl_ref, br_ref)
        xb_ref[...] = x.astype(BF16)
        acc_ref[...] = jnp.zeros_like(acc_ref)

    gu = jnp.dot(xb_ref[...], wgu_ref[0], preferred_element_type=F32)
    h = _silu(gu[:, :MOE_HIDDEN]) * gu[:, MOE_HIDDEN:]
    comb = comb_ref[...]
    lane = lax.broadcasted_iota(jnp.int32, comb.shape, 1)
    c_e = jnp.sum(jnp.where(lane == e + MOE_GROUPS, comb, 0.0), -1, keepdims=True)
    acc_ref[...] += jnp.dot((h * c_e).astype(BF16), wd_ref[0], preferred_element_type=F32)

    @pl.when(e == MOE_EXPERTS - 1)
    def _():
        x2 = _layer_norm(alpha * x_ref[...] + acc_ref[...], g2_ref[...], b2_ref[...])
        gate = jax.nn.sigmoid(jnp.dot(x2.astype(BF16), wpg_ref[...], preferred_element_type=F32))
        ple = gate * jnp.dot(p_ref[...].astype(BF16), wple_ref[...], preferred_element_type=F32)
        o_ref[...] = _layer_norm(alpha * x2 + ple, g3_ref[...], b3_ref[...])


def _moe(x, p, wr_hi, wr_lo, b_r, wgu, wd, g2, b2, wpg, wple, g3, b3, *, alpha):
    m = x.shape[0]
    tm = MOE_TILE
    row = lambda w: pl.BlockSpec((tm, w), lambda i, e: (i, 0))
    full = lambda a: pl.BlockSpec(a.shape, lambda i, e: (0,) * a.ndim)
    return pl.pallas_call(
        functools.partial(_moe_kernel, alpha=alpha), grid=(m // tm, MOE_EXPERTS),
        in_specs=[row(D_MODEL), row(PLE_DIM), full(wr_hi), full(wr_lo), full(b_r),
                  pl.BlockSpec((1, D_MODEL, 2 * MOE_HIDDEN), lambda i, e: (e, 0, 0)),
                  pl.BlockSpec((1, MOE_HIDDEN, D_MODEL), lambda i, e: (e, 0, 0)),
                  full(g2), full(b2), full(wpg), full(wple), full(g3), full(b3)],
        out_specs=row(D_MODEL), out_shape=jax.ShapeDtypeStruct((m, D_MODEL), F32),
        scratch_shapes=[pltpu.VMEM((tm, D_MODEL), F32), pltpu.VMEM((tm, LANES), F32), pltpu.VMEM((tm, D_MODEL), BF16)],
        compiler_params=_params("parallel", "arbitrary"),
        name="moe",
    )(x, p, wr_hi, wr_lo, b_r, wgu, wd, g2, b2, wpg, wple, g3, b3)


def _prep_weights(i, w_in, conv_w, dn_a_log, dn_dt_bias, dn_norm_w, w_dn_out,
                  s5_lam_re, s5_lam_im, s5_log_dt, s5_b_re, s5_b_im, s5_c_re, s5_c_im, s5_d, w_glu,
                  w_out, ln1_g, ln1_b, w_rg, b_rg, w_re, b_re, w_gate, w_up, w_down, ln2_g, ln2_b,
                  w_ple, w_ple_gate, ln3_g, ln3_b):
    o_z = CONV_CH
    o_b = o_z + V_W
    o_u = o_b + 2 * DN_HEADS
    o_ga = o_u + S5_CH
    wi = w_in[i]
    row = lambda a, width=None: a.reshape(1, -1)
    head_pad = lambda a, off: jnp.zeros((1, LANES), F32).at[0, off:off + DN_HEADS].set(a)
    lb_re, lb_im, w_b = _s5_params(s5_lam_re[i], s5_lam_im[i], s5_log_dt[i], s5_b_re[i], s5_b_im[i])
    wc_re, wc_im = _s5_c_weights(s5_c_re[i], s5_c_im[i])
    w_router = jnp.zeros((D_MODEL, LANES), F32)
    w_router = w_router.at[:, :MOE_GROUPS].set(w_rg[i]).at[:, MOE_GROUPS:MOE_GROUPS + MOE_EXPERTS].set(w_re[i])
    b_router = jnp.zeros((1, LANES), F32)
    b_router = b_router.at[0, :MOE_GROUPS].set(b_rg[i]).at[0, MOE_GROUPS:MOE_GROUPS + MOE_EXPERTS].set(b_re[i])
    wr_hi = w_router.astype(BF16)
    wr_lo = (w_router - wr_hi.astype(F32)).astype(BF16)
    return dict(
        wqkv=wi[:, :CONV_CH].astype(BF16), wz=wi[:, o_z:o_b].astype(BF16),
        wba=jnp.pad(wi[:, o_b:o_u], ((0, 0), (0, LANES - 2 * DN_HEADS))).astype(BF16),
        wu=wi[:, o_u:o_ga].astype(BF16), wgab=wi[:, o_ga:].astype(BF16),
        conv_w=conv_w[i], alog=head_pad(dn_a_log[i], DN_HEADS), dtb=head_pad(dn_dt_bias[i], DN_HEADS),
        norm_w=row(dn_norm_w[i]), wdn=w_dn_out[i].astype(BF16),
        lb_re=lb_re, lb_im=lb_im, w_b=w_b, wc_re=wc_re, wc_im=wc_im, s5_d=row(s5_d[i]),
        wglu=w_glu[i].astype(BF16), wout=w_out[i].astype(BF16), ln1_g=row(ln1_g[i]), ln1_b=row(ln1_b[i]),
        wr_hi=wr_hi, wr_lo=wr_lo, b_r=b_router,
        wgu=jnp.concatenate([w_gate[i], w_up[i]], axis=-1).astype(BF16), wd=w_down[i].astype(BF16),
        ln2_g=row(ln2_g[i]), ln2_b=row(ln2_b[i]), wpg=w_ple_gate[i].astype(BF16), wple=w_ple[i].astype(BF16),
        ln3_g=row(ln3_g[i]), ln3_b=row(ln3_b[i]),
    )


def _layer(x, p, conv_buf, s_delta, h_re, h_im, w, alpha):
    bsz, t_len, _ = x.shape
    m = bsz * t_len
    assert t_len >= DN_CONV - 1
    if t_len % ROW_TILE == 0:
        bb, tt = 1, ROW_TILE
        s5_bb, s5_tt = SUBLANES, S5_ROWS // SUBLANES
    else:
        assert ROW_TILE % t_len == 0 and S5_ROWS % t_len == 0
        bb, tt = ROW_TILE // t_len, t_len
        s5_bb, s5_tt = S5_ROWS // t_len, t_len
    x2d = x.reshape(m, D_MODEL)
    q, k, v, gz, bg, u, conv_new = _inproj(x2d, conv_buf, w["wqkv"], w["wz"], w["wba"], w["wu"], w["conv_w"],
                                           w["alog"], w["dtb"], t_len=t_len, bb=bb, tt=tt)
    og, s_new = _delta(q, k, v, gz, bg, s_delta, w["norm_w"], t_len=t_len)
    yb, hr_new, hi_new = _s5(u, h_re.reshape(bsz, S5_LANES), h_im.reshape(bsz, S5_LANES), w["w_b"], w["lb_re"],
                             w["lb_im"], w["wc_re"], w["wc_im"], w["s5_d"], t_len=t_len, bb=s5_bb, tt=s5_tt)
    x1 = _merge(x2d, og, yb, w["wdn"], w["wglu"], w["wgab"], w["wout"], w["ln1_g"], w["ln1_b"], alpha=alpha)
    x3 = _moe(x1, p.reshape(m, PLE_DIM), w["wr_hi"], w["wr_lo"], w["b_r"], w["wgu"], w["wd"], w["ln2_g"],
              w["ln2_b"], w["wpg"], w["wple"], w["ln3_g"], w["ln3_b"], alpha=alpha)
    return (x3.reshape(bsz, t_len, D_MODEL), s_new, conv_new,
            hr_new.reshape(bsz, S5_GROUPS, S5_STATE), hi_new.reshape(bsz, S5_GROUPS, S5_STATE))


def kernel(x_prompt, x_sample, state_delta, state_conv, state_ssm_re, state_ssm_im, p_prompt, p_sample, w_in, conv_w, dn_a_log, dn_dt_bias, dn_norm_w, w_dn_out, s5_lam_re, s5_lam_im, s5_log_dt, s5_b_re, s5_b_im, s5_c_re, s5_c_im, s5_d, w_glu, w_out, ln1_g, ln1_b, w_rg, b_rg, w_re, b_re, w_gate, w_up, w_down, ln2_g, ln2_b, w_ple, w_ple_gate, ln3_g, ln3_b):
    weights = (w_in, conv_w, dn_a_log, dn_dt_bias, dn_norm_w, w_dn_out, s5_lam_re, s5_lam_im, s5_log_dt, s5_b_re,
               s5_b_im, s5_c_re, s5_c_im, s5_d, w_glu, w_out, ln1_g, ln1_b, w_rg, b_rg, w_re, b_re, w_gate, w_up,
               w_down, ln2_g, ln2_b, w_ple, w_ple_gate, ln3_g, ln3_b)
    depth = w_in.shape[0]
    alpha = (2 * depth) ** 0.25
    bp = x_prompt.shape[0]
    y_p, y_s = x_prompt, x_sample
    outs_p, outs_s = [], []
    for i in range(depth):
        w = _prep_weights(i, *weights)
        y_p, *st_p = _layer(y_p, p_prompt[i], jnp.zeros((bp, DN_CONV - 1, CONV_CH), F32),
                            jnp.zeros((bp, DN_HEADS, DN_DK, DN_DV), F32),
                            jnp.zeros((bp, S5_GROUPS, S5_STATE), F32), jnp.zeros((bp, S5_GROUPS, S5_STATE), F32),
                            w, alpha)
        y_s, *st_s = _layer(y_s, p_sample[i], state_conv[i], state_delta[i], state_ssm_re[i], state_ssm_im[i],
                            w, alpha)
        outs_p.append(st_p)
        outs_s.append(st_s)
    stack = lambda outs, j: jnp.stack([o[j] for o in outs])
    return (y_p, y_s, stack(outs_p, 0), stack(outs_p, 1), stack(outs_p, 2), stack(outs_p, 3),
            stack(outs_s, 0), stack(outs_s, 1), stack(outs_s, 2), stack(outs_s, 3))
```

```python
import functools

import jax
import jax.numpy as jnp
import numpy as np
from jax import lax
from jax.experimental import pallas as pl
from jax.experimental.pallas import tpu as pltpu

F32 = jnp.float32
BF16 = jnp.bfloat16

D_MODEL = 1024
DN_HEADS = 8
DN_DK = 128
DN_DV = 128
DN_CONV = 4
DN_CHUNK = 64
QK_W = DN_HEADS * DN_DK
V_W = DN_HEADS * DN_DV
CONV_CH = 2 * QK_W + V_W
S5_CH = 512
S5_GROUP = 16
S5_GROUPS = S5_CH // S5_GROUP
S5_STATE = 64
S5_LANES = S5_GROUPS * S5_STATE
MOE_GROUPS = 4
MOE_PER_GROUP = 4
MOE_EXPERTS = MOE_GROUPS * MOE_PER_GROUP
MOE_HIDDEN = 256
PLE_DIM = 256
LN_EPS = 1e-5
NORM_EPS = 1e-6

LANES = 128
SUBLANES = 8
VMEM_LIMIT_BYTES = 56 * 1024 * 1024

ROW_TILE = 256
MOE_TILE = 512
S5_ROWS = 512
DELTA_ROWS = 256
DELTA_SUB = 128
NEG_BIG = -1e30


def _params(*sem):
    return pltpu.CompilerParams(dimension_semantics=sem, vmem_limit_bytes=VMEM_LIMIT_BYTES)


def _mm(a, b):
    return jnp.dot(a.astype(BF16), b.astype(BF16), preferred_element_type=F32)


def _mm_nt(a, b):
    return lax.dot_general(a.astype(BF16), b.astype(BF16), (((1,), (1,)), ((), ())),
                           preferred_element_type=F32)


def _split3(x):
    x1 = x.astype(BF16)
    r1 = x - x1.astype(F32)
    x2 = r1.astype(BF16)
    r2 = r1 - x2.astype(F32)
    return x1, x2, r2.astype(BF16)


def _split2(x):
    x1 = x.astype(BF16)
    return x1, (x - x1.astype(F32)).astype(BF16)


def _silu(x):
    return x * jax.nn.sigmoid(x)


def _layer_norm(x, g, b):
    mu = jnp.mean(x, -1, keepdims=True)
    xc = x - mu
    var = jnp.mean(xc * xc, -1, keepdims=True)
    return xc * lax.rsqrt(var + LN_EPS) * g + b


def _s5_param_kernel(lre_ref, lim_ref, ldt_ref, bre_ref, bim_ref, lbre_ref, lbim_ref, bbre_ref, bbim_ref):
    lam_re = lre_ref[...]
    lam_im = lim_ref[...]
    dt = jnp.exp(ldt_ref[...])
    mag = jnp.exp(lam_re * dt)
    ang = lam_im * dt
    lb_re = mag * jnp.cos(ang)
    lb_im = mag * jnp.sin(ang)
    den = lam_re * lam_re + lam_im * lam_im
    nr = lb_re - 1.0
    ni = lb_im
    f_re = (nr * lam_re + ni * lam_im) / den
    f_im = (ni * lam_re - nr * lam_im) / den
    b_re = bre_ref[...]
    b_im = bim_ref[...]
    lbre_ref[...] = lb_re
    lbim_ref[...] = lb_im
    bbre_ref[...] = f_re * b_re - f_im * b_im
    bbim_ref[...] = f_re * b_im + f_im * b_re


def _s5_params(lam_re, lam_im, log_dt, b_re, b_im):
    row = lambda a: a.reshape(1, S5_LANES)
    ldt = jnp.repeat(log_dt, S5_STATE).reshape(1, S5_LANES)
    bt = lambda b: jnp.transpose(b, (2, 0, 1)).reshape(S5_GROUP, S5_LANES)
    sds = jax.ShapeDtypeStruct
    lb_re, lb_im, bb_re, bb_im = pl.pallas_call(
        _s5_param_kernel,
        out_shape=(sds((1, S5_LANES), F32), sds((1, S5_LANES), F32),
                   sds((S5_GROUP, S5_LANES), F32), sds((S5_GROUP, S5_LANES), F32)),
    )(row(lam_re), row(lam_im), ldt, bt(b_re), bt(b_im))
    eye = jnp.eye(S5_GROUPS, dtype=F32)

    def block_diag(bb):
        bb = bb.reshape(S5_GROUP, S5_GROUPS, S5_STATE)
        return (eye[:, None, :, None] * bb[None]).reshape(S5_CH, S5_LANES)

    w_b = jnp.concatenate([block_diag(bb_re), block_diag(bb_im)], axis=1).astype(BF16)
    return lb_re, lb_im, w_b


def _s5_c_weights(c_re, c_im):
    eye = jnp.eye(S5_GROUPS, dtype=F32)

    def block_diag(c):
        return (jnp.transpose(c, (0, 2, 1))[:, :, None, :] * eye[:, None, :, None]).reshape(S5_LANES, S5_CH)

    return block_diag(c_re).astype(BF16), block_diag(c_im).astype(BF16)


def _inproj_kernel(x_ref, cbuf_ref, wqkv_ref, wz_ref, wba_ref, wu_ref, convw_ref, alog_ref, dtb_ref,
                   q_ref, k_ref, v_ref, gz_ref, bg_ref, u_ref, cnew_ref, ext_ref, *, bb, tt):
    tm = bb * tt
    pad = SUBLANES
    t = pl.program_id(1)
    xb = x_ref[...].astype(BF16)

    @pl.when(t == 0)
    def _():
        ext_ref[:, pad - 3:pad, :] = cbuf_ref[...]

    pre = jnp.dot(xb, wqkv_ref[...], preferred_element_type=F32)
    ext_ref[:, pad:pad + tt, :] = pre.reshape(bb, tt, CONV_CH)

    for h in range(CONV_CH // LANES):
        cs = slice(h * LANES, (h + 1) * LANES)
        acc = ext_ref[:, pad:pad + tt, cs] * convw_ref[3:4, cs]
        for j in range(DN_CONV - 1):
            acc = acc + ext_ref[:, pad - 3 + j:pad - 3 + j + tt, cs] * convw_ref[j:j + 1, cs]
        y = _silu(acc).reshape(tm, LANES)
        if h < 2 * DN_HEADS:
            y = y * lax.rsqrt(jnp.sum(y * y, -1, keepdims=True) + NORM_EPS)
        if h < DN_HEADS:
            q_ref[:, cs] = y * (DN_DK ** -0.5)
        elif h < 2 * DN_HEADS:
            k_ref[:, slice(cs.start - QK_W, cs.stop - QK_W)] = y
        else:
            v_ref[:, slice(cs.start - 2 * QK_W, cs.stop - 2 * QK_W)] = y

    carry = ext_ref[:, pad + tt - 3:pad + tt, :]
    cnew_ref[...] = carry
    ext_ref[:, pad - 3:pad, :] = carry

    z = jnp.dot(xb, wz_ref[...], preferred_element_type=F32)
    gz_ref[...] = _silu(z)
    u_ref[...] = jnp.dot(xb, wu_ref[...], preferred_element_type=F32)

    ba = jnp.dot(xb, wba_ref[...], preferred_element_type=F32)
    lane = lax.broadcasted_iota(jnp.int32, ba.shape, 1)
    beta = jax.nn.sigmoid(ba)
    sp_in = ba + dtb_ref[...]
    softplus = jnp.maximum(sp_in, 0.0) + jnp.log1p(jnp.exp(-jnp.abs(sp_in)))
    g = -jnp.exp(alog_ref[...]) * softplus
    bg = jnp.where(lane < DN_HEADS, beta, jnp.where(lane < 2 * DN_HEADS, g, 0.0))
    bg_ref[...] = bg


def _inproj(x, conv_buf, wqkv, wz, wba, wu, conv_w, alog_row, dtb_row, *, t_len, bb, tt):
    m = x.shape[0]
    bsz = m // t_len
    n_t = t_len // tt
    grid = (bsz // bb, n_t)
    tok = lambda w: pl.BlockSpec((bb * tt, w), lambda b, t: (b * n_t + t, 0))
    full = lambda a: pl.BlockSpec(a.shape, lambda b, t: (0,) * a.ndim)
    sds = jax.ShapeDtypeStruct
    return pl.pallas_call(
        functools.partial(_inproj_kernel, bb=bb, tt=tt),
        grid=grid,
        in_specs=[tok(D_MODEL), pl.BlockSpec((bb, DN_CONV - 1, CONV_CH), lambda b, t: (b, 0, 0)),
                  full(wqkv), full(wz), full(wba), full(wu), full(conv_w), full(alog_row), full(dtb_row)],
        out_specs=[tok(QK_W), tok(QK_W), tok(V_W), tok(V_W), tok(LANES), tok(S5_CH),
                   pl.BlockSpec((bb, DN_CONV - 1, CONV_CH), lambda b, t: (b, 0, 0))],
        out_shape=[sds((m, QK_W), F32), sds((m, QK_W), F32), sds((m, V_W), F32), sds((m, V_W), F32),
                   sds((m, LANES), F32), sds((m, S5_CH), F32), sds((bsz, DN_CONV - 1, CONV_CH), F32)],
        scratch_shapes=[pltpu.VMEM((bb, SUBLANES + tt, CONV_CH), F32)],
        compiler_params=_params("parallel", "arbitrary"),
        name="inproj",
    )(x, conv_buf, wqkv, wz, wba, wu, conv_w, alog_row, dtb_row)


def _delta_intra(q, k, v, bgv, head, chunk):
    rows = DELTA_ROWS
    lane = lax.broadcasted_iota(jnp.int32, (rows, LANES), 1)
    beta = jnp.sum(jnp.where(lane == head, bgv, 0.0), -1, keepdims=True)
    g = jnp.sum(jnp.where(lane == head + DN_HEADS, bgv, 0.0), -1, keepdims=True)

    ri = lax.broadcasted_iota(jnp.int32, (rows, rows), 0)
    ci = lax.broadcasted_iota(jnp.int32, (rows, rows), 1)
    shift = chunk.bit_length() - 1
    same = (ri >> shift) == (ci >> shift)
    incl = same & (ri >= ci)
    strict = same & (ri > ci)

    gb = jnp.broadcast_to(g, (rows, LANES))
    sel = jnp.concatenate([jnp.where(incl, 1.0, 0.0), jnp.where(same, 1.0, 0.0)], axis=0).astype(BF16)
    g1, g2, g3 = _split3(gb)
    cs = (jnp.dot(sel, g1, preferred_element_type=F32) + jnp.dot(sel, g2, preferred_element_type=F32)
          + jnp.dot(sel, g3, preferred_element_type=F32))
    gc = cs[:rows]
    g_last = cs[rows:]
    egc = jnp.exp(gc)

    gc_sq = jnp.concatenate([gc, gc], axis=1)
    diff = gc_sq - gc_sq.T
    decay = jnp.where(incl, jnp.exp(jnp.where(incl, diff, 0.0)), 0.0)

    kb = k * beta
    a_mat = jnp.where(strict, _mm_nt(kb, k) * decay, 0.0)
    x_mat = -a_mat
    p_mat = a_mat
    for _ in range(shift - 1):
        p_mat = _mm(p_mat, p_mat)
        x_mat = x_mat + p_mat + _mm(x_mat, p_mat)
    rhs = jnp.concatenate([v * beta, kb * egc], axis=1)
    uw = rhs + _mm(x_mat, rhs)
    u = uw[:, :DN_DV]
    w = uw[:, DN_DV:]
    qk = jnp.where(incl, _mm_nt(q, k) * decay, 0.0)
    q_dec = q * egc
    k_dec = k * jnp.exp(g_last - gc)
    return u, w, qk, q_dec, k_dec, jnp.exp(g_last)


def _delta_finish(o, gz, nw_ref, og_ref):
    o = o * lax.rsqrt(jnp.mean(o * o, -1, keepdims=True) + NORM_EPS) * nw_ref[...]
    og_ref[...] = (o * gz).astype(BF16)


def _delta_prompt_kernel(q_ref, k_ref, v_ref, gz_ref, bg_ref, s0_ref, nw_ref, og_ref, sout_ref, s_ref):
    rows, c, sb = DELTA_ROWS, DN_CHUNK, DELTA_SUB
    n_sub, cps = rows // sb, sb // c
    shift = c.bit_length() - 1
    t = pl.program_id(1)

    @pl.when(t == 0)
    def _():
        s_ref[...] = s0_ref[0]

    bgv = bg_ref[...]
    ri = lax.broadcasted_iota(jnp.int32, (rows, rows), 0)
    ci = lax.broadcasted_iota(jnp.int32, (rows, rows), 1)
    same = (ri >> shift) == (ci >> shift)
    sel = jnp.concatenate([jnp.where(same & (ri >= ci), 1.0, 0.0), jnp.where(same, 1.0, 0.0)], axis=0).astype(BF16)
    g1, g2, g3 = _split3(bgv)
    cs = (jnp.dot(sel, g1, preferred_element_type=F32) + jnp.dot(sel, g2, preferred_element_type=F32)
          + jnp.dot(sel, g3, preferred_element_type=F32))
    gcs = cs[:rows]
    gls = cs[rows:]
    gcs_t = gcs.T

    rs = lax.broadcasted_iota(jnp.int32, (sb, sb), 0)
    cc = lax.broadcasted_iota(jnp.int32, (sb, sb), 1)
    same_s = (rs >> shift) == (cc >> shift)
    incl = same_s & (rs >= cc)
    strict = same_s & (rs > cc)

    probs = [(h, s) for s in range(n_sub) for h in range(DN_HEADS)]
    bcast = lambda col: jnp.broadcast_to(col, (sb, LANES))
    q, k, kbeta, decay, rhs, q_dec, kd_t, g_tot = {}, {}, {}, {}, {}, {}, {}, {}
    for p in probs:
        h, s = p
        rsl = slice(s * sb, (s + 1) * sb)
        hs = slice(h * LANES, (h + 1) * LANES)
        lane_g = DN_HEADS + h
        beta = bcast(bgv[rsl, h:h + 1])
        gc = bcast(gcs[rsl, lane_g:lane_g + 1])
        gl = bcast(gls[rsl, lane_g:lane_g + 1])
        egc = jnp.exp(gc)
        q[p] = q_ref[rsl, hs]
        k[p] = k_ref[rsl, hs]
        kbeta[p] = k[p] * beta
        decay[p] = jnp.where(incl, jnp.exp(jnp.where(incl, gc - gcs_t[lane_g:lane_g + 1, rsl], 0.0)), 0.0)
        rhs[p] = jnp.concatenate([v_ref[rsl, hs] * beta, kbeta[p] * egc], axis=1)
        q_dec[p] = q[p] * egc
        kd_t[p] = (k[p] * jnp.exp(gl - gc)).T
        g_tot[p] = jnp.exp(gl)
    a_mat = {p: jnp.where(strict, _mm_nt(kbeta[p], k[p]) * decay[p], 0.0) for p in probs}
    qk = {p: jnp.where(incl, _mm_nt(q[p], k[p]) * decay[p], 0.0) for p in probs}
    x_mat = {p: -a_mat[p] for p in probs}
    p_mat = a_mat
    for _ in range(shift - 1):
        p_mat = {p: _mm(p_mat[p], p_mat[p]) for p in probs}
        xp = {p: _mm(x_mat[p], p_mat[p]) for p in probs}
        x_mat = {p: x_mat[p] + p_mat[p] + xp[p] for p in probs}
    uw = {p: rhs[p] + _mm(x_mat[p], rhs[p]) for p in probs}

    col = lax.broadcasted_iota(jnp.int32, (DN_DK, sb), 1)
    zeros = jnp.zeros((c, DN_DV), F32)
    state = [s_ref[h] for h in range(DN_HEADS)]
    vn_parts = {p: [] for p in probs}
    qs_parts = {p: [] for p in probs}
    for j in range(rows // c):
        s, jj = divmod(j, cps)
        sl = slice(jj * c, (jj + 1) * c)
        r = {h: _mm(jnp.concatenate([uw[(h, s)][sl, DN_DV:], q_dec[(h, s)][sl]], axis=0), state[h])
             for h in range(DN_HEADS)}
        for h in range(DN_HEADS):
            p = (h, s)
            vn = uw[p][sl, :DN_DV] - r[h][:c]
            vn_parts[p].append(vn)
            qs_parts[p].append(r[h][c:])
            vn_pad = jnp.concatenate([vn if i == jj else zeros for i in range(cps)], axis=0)
            kd_j = jnp.where((col >= jj * c) & (col < (jj + 1) * c), kd_t[p], 0.0)
            state[h] = state[h] * g_tot[p][jj * c:jj * c + 1, :] + _mm(kd_j, vn_pad)
    for h in range(DN_HEADS):
        s_ref[h] = state[h]
    for p in probs:
        h, s = p
        o = jnp.concatenate(qs_parts[p], axis=0) + _mm(qk[p], jnp.concatenate(vn_parts[p], axis=0))
        o = o * lax.rsqrt(jnp.mean(o * o, -1, keepdims=True) + NORM_EPS) * nw_ref[...]
        rsl = slice(s * sb, (s + 1) * sb)
        hs = slice(h * LANES, (h + 1) * LANES)
        og_ref[rsl, hs] = (o * gz_ref[rsl, hs]).astype(BF16)

    @pl.when(t == pl.num_programs(1) - 1)
    def _():
        sout_ref[0] = s_ref[...]


def _delta_sample_kernel(q_ref, k_ref, v_ref, gz_ref, bg_ref, s0_ref, nw_ref, og_ref, sout_ref, *, tt):
    rows, c = DELTA_ROWS, tt
    nseq = rows // c
    head = pl.program_id(1)
    u, w, qk, q_dec, k_dec, g_tot = _delta_intra(q_ref[...], k_ref[...], v_ref[...], bg_ref[...], head, c)
    kd_t = k_dec.T
    col = lax.broadcasted_iota(jnp.int32, kd_t.shape, 1)
    row = lax.broadcasted_iota(jnp.int32, (rows, DN_DV), 0)
    vn_parts, qs_parts = [], []
    for i in range(nseq):
        sl = slice(i * c, (i + 1) * c)
        s = s0_ref[i, 0]
        r = _mm(jnp.concatenate([w[sl], q_dec[sl]], axis=0), s)
        vn_parts.append(u[sl] - r[:c])
        qs_parts.append(r[c:])
    vn_all = jnp.concatenate(vn_parts, axis=0)
    for i in range(nseq):
        kd_i = jnp.where((col >= i * c) & (col < (i + 1) * c), kd_t, 0.0)
        sout_ref[i, 0] = s0_ref[i, 0] * g_tot[i * c:i * c + 1, :] + _mm(kd_i, vn_all)
    o = jnp.concatenate(qs_parts, axis=0) + _mm(qk, vn_all)
    _delta_finish(o, gz_ref[...], nw_ref, og_ref)


def _delta(q, k, v, gz, bg, s0, norm_w, *, t_len):
    m = q.shape[0]
    bsz = m // t_len
    sds = jax.ShapeDtypeStruct
    out_shape = [sds((m, V_W), BF16), sds((bsz, DN_HEADS, DN_DK, DN_DV), F32)]
    if t_len % DELTA_ROWS == 0:
        n_t = t_len // DELTA_ROWS
        grid = (bsz, n_t)
        tok = lambda w: pl.BlockSpec((DELTA_ROWS, w), lambda b, t: (b * n_t + t, 0))
        st = pl.BlockSpec((1, DN_HEADS, DN_DK, DN_DV), lambda b, t: (b, 0, 0, 0))
        return pl.pallas_call(
            _delta_prompt_kernel, grid=grid,
            in_specs=[tok(QK_W), tok(QK_W), tok(V_W), tok(V_W), tok(LANES), st,
                      pl.BlockSpec((1, LANES), lambda b, t: (0, 0))],
            out_specs=[tok(V_W), st], out_shape=out_shape,
            scratch_shapes=[pltpu.VMEM((DN_HEADS, DN_DK, DN_DV), F32)],
            compiler_params=_params("parallel", "arbitrary"),
            name="delta_prompt",
        )(q, k, v, gz, bg, s0, norm_w)
    assert t_len <= DN_CHUNK and DELTA_ROWS % t_len == 0 and t_len % SUBLANES == 0
    nseq = DELTA_ROWS // t_len
    assert bsz % nseq == 0
    grid = (bsz // nseq, DN_HEADS)
    hd = pl.BlockSpec((DELTA_ROWS, LANES), lambda b, h: (b, h))
    st = pl.BlockSpec((nseq, 1, DN_DK, DN_DV), lambda b, h: (b, h, 0, 0))
    return pl.pallas_call(
        functools.partial(_delta_sample_kernel, tt=t_len), grid=grid,
        in_specs=[hd, hd, hd, hd, pl.BlockSpec((DELTA_ROWS, LANES), lambda b, h: (b, 0)), st,
                  pl.BlockSpec((1, LANES), lambda b, h: (0, 0))],
        out_specs=[hd, st], out_shape=out_shape,
        compiler_params=_params("parallel", "parallel"),
        name="delta_sample",
    )(q, k, v, gz, bg, s0, norm_w)


def _s5_perm(bb, tt):
    i = np.arange(bb * tt)
    gi, r = np.divmod(i, SUBLANES * tt)
    ti, bl = np.divmod(r, SUBLANES)
    perm = np.zeros((bb * tt, bb * tt), np.float32)
    perm[i, gi * SUBLANES * tt + bl * tt + ti] = 1.0
    return perm


def _s5_kernel(u_ref, h0re_ref, h0im_ref, perm_ref, permt_ref, wb_ref, lbre_ref, lbim_ref, wcre_ref, wcim_ref, d_ref,
               yb_ref, hre_ref, him_ref, hbuf_ref, hst_ref, *, bb, tt):
    rows = bb * tt
    t = pl.program_id(1)
    n_tiles = S5_LANES // LANES
    perm = perm_ref[...]
    u2 = sum(jnp.dot(perm, piece, preferred_element_type=F32) for piece in _split3(u_ref[...].reshape(rows, S5_CH)))
    bu = jnp.dot(u2.astype(BF16), wb_ref[...], preferred_element_type=F32)
    for c in range(2 * n_tiles):
        hbuf_ref[c] = bu[:, c * LANES:(c + 1) * LANES]

    @pl.when(t == 0)
    def _():
        hst_ref[:, :, :S5_LANES] = h0re_ref[...].reshape(bb // SUBLANES, SUBLANES, S5_LANES)
        hst_ref[:, :, S5_LANES:] = h0im_ref[...].reshape(bb // SUBLANES, SUBLANES, S5_LANES)

    blk = 8
    for gi in range(bb // SUBLANES):
        for c0 in range(0, n_tiles, blk):
            lanes = [slice((c0 + j) * LANES, (c0 + j + 1) * LANES) for j in range(blk)]
            a_re = [lbre_ref[:, s] for s in lanes]
            a_im = [lbim_ref[:, s] for s in lanes]

            def step(i, carry):
                idx = pl.ds(pl.multiple_of((gi * tt + i) * SUBLANES, SUBLANES), SUBLANES)
                new = []
                for j in range(blk):
                    h_re, h_im = carry[j]
                    n_re = a_re[j] * h_re - a_im[j] * h_im + hbuf_ref[c0 + j, idx, :]
                    n_im = a_re[j] * h_im + a_im[j] * h_re + hbuf_ref[n_tiles + c0 + j, idx, :]
                    hbuf_ref[c0 + j, idx, :] = n_re
                    hbuf_ref[n_tiles + c0 + j, idx, :] = n_im
                    new.append((n_re, n_im))
                return tuple(new)

            init = tuple((hst_ref[gi, :, s], hst_ref[gi, :, slice(S5_LANES + s.start, S5_LANES + s.stop)])
                         for s in lanes)
            fin = lax.fori_loop(0, tt, step, init)
            for j, s in enumerate(lanes):
                hst_ref[gi, :, s] = fin[j][0]
                hst_ref[gi, :, slice(S5_LANES + s.start, S5_LANES + s.stop)] = fin[j][1]

    h_re_all = jnp.concatenate([hbuf_ref[c] for c in range(n_tiles)], axis=1)
    h_im_all = jnp.concatenate([hbuf_ref[n_tiles + c] for c in range(n_tiles)], axis=1)
    y = (jnp.dot(h_re_all.astype(BF16), wcre_ref[...], preferred_element_type=F32)
         - jnp.dot(h_im_all.astype(BF16), wcim_ref[...], preferred_element_type=F32))
    y = y + d_ref[...] * u2
    yb = jnp.dot(permt_ref[...], jax.nn.gelu(y).astype(BF16), preferred_element_type=F32)
    yb_ref[...] = yb.reshape(yb_ref.shape).astype(BF16)
    hre_ref[...] = hst_ref[:, :, :S5_LANES].reshape(bb, S5_LANES)
    him_ref[...] = hst_ref[:, :, S5_LANES:].reshape(bb, S5_LANES)


def _s5(u, h0_re, h0_im, w_b, lb_re, lb_im, wc_re, wc_im, d_row, *, t_len, bb, tt):
    m = u.shape[0]
    bsz = m // t_len
    grid = (bsz // bb, t_len // tt)
    if tt == t_len:
        view = (1, m, S5_CH)
        tok = pl.BlockSpec((1, bb * tt, S5_CH), lambda b, t: (0, b, 0))
    else:
        view = (bsz, t_len, S5_CH)
        tok = pl.BlockSpec((bb, tt, S5_CH), lambda b, t: (b, t, 0))
    full = lambda a: pl.BlockSpec(a.shape, lambda b, t: (0,) * a.ndim)
    st = pl.BlockSpec((bb, S5_LANES), lambda b, t: (b, 0))
    sds = jax.ShapeDtypeStruct
    perm = _s5_perm(bb, tt)
    perm, perm_t = jnp.asarray(perm, BF16), jnp.asarray(perm.T, BF16)
    yb, h_re, h_im = pl.pallas_call(
        functools.partial(_s5_kernel, bb=bb, tt=tt), grid=grid,
        in_specs=[tok, st, st, full(perm), full(perm_t), full(w_b), full(lb_re), full(lb_im), full(wc_re),
                  full(wc_im), full(d_row)],
        out_specs=[tok, st, st],
        out_shape=[sds(view, BF16), sds((bsz, S5_LANES), F32), sds((bsz, S5_LANES), F32)],
        scratch_shapes=[pltpu.VMEM((2 * S5_LANES // LANES, bb * tt, LANES), F32),
                        pltpu.VMEM((bb // SUBLANES, SUBLANES, 2 * S5_LANES), F32)],
        compiler_params=_params("parallel", "arbitrary"),
        name="s5",
    )(u.reshape(view), h0_re, h0_im, perm, perm_t, w_b, lb_re, lb_im, wc_re, wc_im, d_row)
    return yb.reshape(m, S5_CH), h_re, h_im


def _merge_kernel(x_ref, og_ref, yb_ref, wdn_ref, wglu_ref, wgab_ref, wout_ref, g_ref, b_ref, o_ref, *, alpha):
    x = x_ref[...]
    branch_a = jnp.dot(og_ref[...], wdn_ref[...], preferred_element_type=F32)
    glu = jnp.dot(yb_ref[...], wglu_ref[...], preferred_element_type=F32)
    branch_b = glu[:, :D_MODEL] * jax.nn.sigmoid(glu[:, D_MODEL:])
    gab = jnp.dot(x.astype(BF16), wgab_ref[...], preferred_element_type=F32)
    mix_in = jax.nn.sigmoid(gab[:, :D_MODEL]) * branch_a + jax.nn.sigmoid(gab[:, D_MODEL:]) * branch_b
    mix = jnp.dot(mix_in.astype(BF16), wout_ref[...], preferred_element_type=F32)
    o_ref[...] = _layer_norm(alpha * x + mix, g_ref[...], b_ref[...])


def _merge(x, og, yb, wdn, wglu, wgab, wout, ln_g, ln_b, *, alpha):
    m = x.shape[0]
    tm = ROW_TILE
    row = lambda w: pl.BlockSpec((tm, w), lambda i: (i, 0))
    full = lambda a: pl.BlockSpec(a.shape, lambda i: (0,) * a.ndim)
    return pl.pallas_call(
        functools.partial(_merge_kernel, alpha=alpha), grid=(m // tm,),
        in_specs=[row(D_MODEL), row(V_W), row(S5_CH), full(wdn), full(wglu), full(wgab), full(wout),
                  full(ln_g), full(ln_b)],
        out_specs=row(D_MODEL), out_shape=jax.ShapeDtypeStruct((m, D_MODEL), F32),
        compiler_params=_params("parallel"),
        name="merge",
    )(x, og, yb, wdn, wglu, wgab, wout, ln_g, ln_b)


def _route(x, wr_hi_ref, wr_lo_ref, br_ref):
    x_hi, x_lo = _split2(x)
    logits = (jnp.dot(x_hi, wr_hi_ref[...], preferred_element_type=F32)
              + jnp.dot(x_hi, wr_lo_ref[...], preferred_element_type=F32)
              + jnp.dot(x_lo, wr_hi_ref[...], preferred_element_type=F32)) + br_ref[...]
    lane = lax.broadcasted_iota(jnp.int32, logits.shape, 1).astype(F32)
    first = lambda mask: jnp.min(jnp.where(mask, lane, float(LANES)), -1, keepdims=True)
    lg = jnp.where(lane < MOE_GROUPS, logits, NEG_BIG)
    g_max = jnp.max(lg, -1, keepdims=True)
    g_idx = first(lg == g_max)
    g_w = 1.0 / jnp.sum(jnp.where(lane < MOE_GROUPS, jnp.exp(lg - g_max), 0.0), -1, keepdims=True)
    lo = MOE_GROUPS + MOE_PER_GROUP * g_idx
    le = jnp.where((lane >= lo) & (lane < lo + MOE_PER_GROUP), logits, NEG_BIG)
    m1 = jnp.max(le, -1, keepdims=True)
    i1 = first(le == m1)
    le2 = jnp.where(lane == i1, NEG_BIG, le)
    m2 = jnp.max(le2, -1, keepdims=True)
    i2 = first(le2 == m2)
    e2 = jnp.exp(m2 - m1)
    w1 = g_w / (1.0 + e2)
    w2 = g_w * e2 / (1.0 + e2)
    return jnp.where(lane == i1, w1, 0.0) + jnp.where(lane == i2, w2, 0.0)


def _moe_kernel(x_ref, p_ref, wrh_ref, wrl_ref, br_ref, wgu_ref, wd_ref, g2_ref, b2_ref, wpg_ref, wple_ref,
                g3_ref, b3_ref, o_ref, acc_ref, comb_ref, xb_ref, *, alpha):
    e = pl.program_id(1)

    @pl.when(e == 0)
    def _():
        x = x_ref[...]
        comb_ref[...] = _route(x, wrh_ref, wrl_ref, br_ref)
        xb_ref[...] = x.astype(BF16)
        acc_ref[...] = jnp.zeros_like(acc_ref)

    gu = jnp.dot(xb_ref[...], wgu_ref[0], preferred_element_type=F32)
    h = _silu(gu[:, :MOE_HIDDEN]) * gu[:, MOE_HIDDEN:]
    comb = comb_ref[...]
    lane = lax.broadcasted_iota(jnp.int32, comb.shape, 1)
    c_e = jnp.sum(jnp.where(lane == e + MOE_GROUPS, comb, 0.0), -1, keepdims=True)
    acc_ref[...] += jnp.dot((h * c_e).astype(BF16), wd_ref[0], preferred_element_type=F32)

    @pl.when(e == MOE_EXPERTS - 1)
    def _():
        x2 = _layer_norm(alpha * x_ref[...] + acc_ref[...], g2_ref[...], b2_ref[...])
        gate = jax.nn.sigmoid(jnp.dot(x2.astype(BF16), wpg_ref[...], preferred_element_type=F32))
        ple = gate * jnp.dot(p_ref[...].astype(BF16), wple_ref[...], preferred_element_type=F32)
        o_ref[...] = _layer_norm(alpha * x2 + ple, g3_ref[...], b3_ref[...])


def _moe(x, p, wr_hi, wr_lo, b_r, wgu, wd, g2, b2, wpg, wple, g3, b3, *, alpha):
    m = x.shape[0]
    tm = MOE_TILE
    row = lambda w: pl.BlockSpec((tm, w), lambda i, e: (i, 0))
    full = lambda a: pl.BlockSpec(a.shape, lambda i, e: (0,) * a.ndim)
    return pl.pallas_call(
        functools.partial(_moe_kernel, alpha=alpha), grid=(m // tm, MOE_EXPERTS),
        in_specs=[row(D_MODEL), row(PLE_DIM), full(wr_hi), full(wr_lo), full(b_r),
                  pl.BlockSpec((1, D_MODEL, 2 * MOE_HIDDEN), lambda i, e: (e, 0, 0)),
                  pl.BlockSpec((1, MOE_HIDDEN, D_MODEL), lambda i, e: (e, 0, 0)),
                  full(g2), full(b2), full(wpg), full(wple), full(g3), full(b3)],
        out_specs=row(D_MODEL), out_shape=jax.ShapeDtypeStruct((m, D_MODEL), F32),
        scratch_shapes=[pltpu.VMEM((tm, D_MODEL), F32), pltpu.VMEM((tm, LANES), F32), pltpu.VMEM((tm, D_MODEL), BF16)],
        compiler_params=_params("parallel", "arbitrary"),
        name="moe",
    )(x, p, wr_hi, wr_lo, b_r, wgu, wd, g2, b2, wpg, wple, g3, b3)


def _prep_weights(i, w_in, conv_w, dn_a_log, dn_dt_bias, dn_norm_w, w_dn_out,
                  s5_lam_re, s5_lam_im, s5_log_dt, s5_b_re, s5_b_im, s5_c_re, s5_c_im, s5_d, w_glu,
                  w_out, ln1_g, ln1_b, w_rg, b_rg, w_re, b_re, w_gate, w_up, w_down, ln2_g, ln2_b,
                  w_ple, w_ple_gate, ln3_g, ln3_b):
    o_z = CONV_CH
    o_b = o_z + V_W
    o_u = o_b + 2 * DN_HEADS
    o_ga = o_u + S5_CH
    wi = w_in[i]
    row = lambda a, width=None: a.reshape(1, -1)
    head_pad = lambda a, off: jnp.zeros((1, LANES), F32).at[0, off:off + DN_HEADS].set(a)
    lb_re, lb_im, w_b = _s5_params(s5_lam_re[i], s5_lam_im[i], s5_log_dt[i], s5_b_re[i], s5_b_im[i])
    wc_re, wc_im = _s5_c_weights(s5_c_re[i], s5_c_im[i])
    w_router = jnp.zeros((D_MODEL, LANES), F32)
    w_router = w_router.at[:, :MOE_GROUPS].set(w_rg[i]).at[:, MOE_GROUPS:MOE_GROUPS + MOE_EXPERTS].set(w_re[i])
    b_router = jnp.zeros((1, LANES), F32)
    b_router = b_router.at[0, :MOE_GROUPS].set(b_rg[i]).at[0, MOE_GROUPS:MOE_GROUPS + MOE_EXPERTS].set(b_re[i])
    wr_hi = w_router.astype(BF16)
    wr_lo = (w_router - wr_hi.astype(F32)).astype(BF16)
    return dict(
        wqkv=wi[:, :CONV_CH].astype(BF16), wz=wi[:, o_z:o_b].astype(BF16),
        wba=jnp.pad(wi[:, o_b:o_u], ((0, 0), (0, LANES - 2 * DN_HEADS))).astype(BF16),
        wu=wi[:, o_u:o_ga].astype(BF16), wgab=wi[:, o_ga:].astype(BF16),
        conv_w=conv_w[i], alog=head_pad(dn_a_log[i], DN_HEADS), dtb=head_pad(dn_dt_bias[i], DN_HEADS),
        norm_w=row(dn_norm_w[i]), wdn=w_dn_out[i].astype(BF16),
        lb_re=lb_re, lb_im=lb_im, w_b=w_b, wc_re=wc_re, wc_im=wc_im, s5_d=row(s5_d[i]),
        wglu=w_glu[i].astype(BF16), wout=w_out[i].astype(BF16), ln1_g=row(ln1_g[i]), ln1_b=row(ln1_b[i]),
        wr_hi=wr_hi, wr_lo=wr_lo, b_r=b_router,
        wgu=jnp.concatenate([w_gate[i], w_up[i]], axis=-1).astype(BF16), wd=w_down[i].astype(BF16),
        ln2_g=row(ln2_g[i]), ln2_b=row(ln2_b[i]), wpg=w_ple_gate[i].astype(BF16), wple=w_ple[i].astype(BF16),
        ln3_g=row(ln3_g[i]), ln3_b=row(ln3_b[i]),
    )


def _layer(x, p, conv_buf, s_delta, h_re, h_im, w, alpha):
    bsz, t_len, _ = x.shape
    m = bsz * t_len
    assert t_len >= DN_CONV - 1
    if t_len % ROW_TILE == 0:
        bb, tt = 1, ROW_TILE
        s5_bb, s5_tt = SUBLANES, S5_ROWS // SUBLANES
    else:
        assert ROW_TILE % t_len == 0 and S5_ROWS % t_len == 0
        bb, tt = ROW_TILE // t_len, t_len
        s5_bb, s5_tt = S5_ROWS // t_len, t_len
    x2d = x.reshape(m, D_MODEL)
    q, k, v, gz, bg, u, conv_new = _inproj(x2d, conv_buf, w["wqkv"], w["wz"], w["wba"], w["wu"], w["conv_w"],
                                           w["alog"], w["dtb"], t_len=t_len, bb=bb, tt=tt)
    og, s_new = _delta(q, k, v, gz, bg, s_delta, w["norm_w"], t_len=t_len)
    yb, hr_new, hi_new = _s5(u, h_re.reshape(bsz, S5_LANES), h_im.reshape(bsz, S5_LANES), w["w_b"], w["lb_re"],
                             w["lb_im"], w["wc_re"], w["wc_im"], w["s5_d"], t_len=t_len, bb=s5_bb, tt=s5_tt)
    x1 = _merge(x2d, og, yb, w["wdn"], w["wglu"], w["wgab"], w["wout"], w["ln1_g"], w["ln1_b"], alpha=alpha)
    x3 = _moe(x1, p.reshape(m, PLE_DIM), w["wr_hi"], w["wr_lo"], w["b_r"], w["wgu"], w["wd"], w["ln2_g"],
              w["ln2_b"], w["wpg"], w["wple"], w["ln3_g"], w["ln3_b"], alpha=alpha)
    return (x3.reshape(bsz, t_len, D_MODEL), s_new, conv_new,
            hr_new.reshape(bsz, S5_GROUPS, S5_STATE), hi_new.reshape(bsz, S5_GROUPS, S5_STATE))


def kernel(x_prompt, x_sample, state_delta, state_conv, state_ssm_re, state_ssm_im, p_prompt, p_sample, w_in, conv_w, dn_a_log, dn_dt_bias, dn_norm_w, w_dn_out, s5_lam_re, s5_lam_im, s5_log_dt, s5_b_re, s5_b_im, s5_c_re, s5_c_im, s5_d, w_glu, w_out, ln1_g, ln1_b, w_rg, b_rg, w_re, b_re, w_gate, w_up, w_down, ln2_g, ln2_b, w_ple, w_ple_gate, ln3_g, ln3_b):
    weights = (w_in, conv_w, dn_a_log, dn_dt_bias, dn_norm_w, w_dn_out, s5_lam_re, s5_lam_im, s5_log_dt, s5_b_re,
               s5_b_im, s5_c_re, s5_c_im, s5_d, w_glu, w_out, ln1_g, ln1_b, w_rg, b_rg, w_re, b_re, w_gate, w_up,
               w_down, ln2_g, ln2_b, w_ple, w_ple_gate, ln3_g, ln3_b)
    depth = w_in.shape[0]
    alpha = (2 * depth) ** 0.25
    bp = x_prompt.shape[0]
    y_p, y_s = x_prompt, x_sample
    outs_p, outs_s = [], []
    for i in range(depth):
        w = _prep_weights(i, *weights)
        y_p, *st_p = _layer(y_p, p_prompt[i], jnp.zeros((bp, DN_CONV - 1, CONV_CH), F32),
                            jnp.zeros((bp, DN_HEADS, DN_DK, DN_DV), F32),
                            jnp.zeros((bp, S5_GROUPS, S5_STATE), F32), jnp.zeros((bp, S5_GROUPS, S5_STATE), F32),
                            w, alpha)
        y_s, *st_s = _layer(y_s, p_sample[i], state_conv[i], state_delta[i], state_ssm_re[i], state_ssm_im[i],
                            w, alpha)
        outs_p.append(st_p)
        outs_s.append(st_s)
    stack = lambda outs, j: jnp.stack([o[j] for o in outs])
    return (y_p, y_s, stack(outs_p, 0), stack(outs_p, 1), stack(outs_p, 2), stack(outs_p, 3),
            stack(outs_s, 0), stack(outs_s, 1), stack(outs_s, 2), stack(outs_s, 3))
```

```python
import functools

import jax
import jax.numpy as jnp
import numpy as np
from jax import lax
from jax.experimental import pallas as pl
from jax.experimental.pallas import tpu as pltpu

F32 = jnp.float32
BF16 = jnp.bfloat16

D_MODEL = 1024
DN_HEADS = 8
DN_DK = 128
DN_DV = 128
DN_CONV = 4
DN_CHUNK = 64
QK_W = DN_HEADS * DN_DK
V_W = DN_HEADS * DN_DV
CONV_CH = 2 * QK_W + V_W
S5_CH = 512
S5_GROUP = 16
S5_GROUPS = S5_CH // S5_GROUP
S5_STATE = 64
S5_LANES = S5_GROUPS * S5_STATE
MOE_GROUPS = 4
MOE_PER_GROUP = 4
MOE_EXPERTS = MOE_GROUPS * MOE_PER_GROUP
MOE_HIDDEN = 256
PLE_DIM = 256
LN_EPS = 1e-5
NORM_EPS = 1e-6

LANES = 128
SUBLANES = 8
VMEM_LIMIT_BYTES = 56 * 1024 * 1024

ROW_TILE = 256
MOE_TILE = 512
MOE_PER_STEP = 4
S5_ROWS = 512
DELTA_ROWS = 256
DELTA_SUB = 128
NEG_BIG = -1e30


def _params(*sem):
    return pltpu.CompilerParams(dimension_semantics=sem, vmem_limit_bytes=VMEM_LIMIT_BYTES)


def _mm(a, b):
    return jnp.dot(a.astype(BF16), b.astype(BF16), preferred_element_type=F32)


def _mm_nt(a, b):
    return lax.dot_general(a.astype(BF16), b.astype(BF16), (((1,), (1,)), ((), ())),
                           preferred_element_type=F32)


def _split3(x):
    x1 = x.astype(BF16)
    r1 = x - x1.astype(F32)
    x2 = r1.astype(BF16)
    r2 = r1 - x2.astype(F32)
    return x1, x2, r2.astype(BF16)


def _split2(x):
    x1 = x.astype(BF16)
    return x1, (x - x1.astype(F32)).astype(BF16)


def _silu(x):
    return x * jax.nn.sigmoid(x)


def _layer_norm(x, g, b):
    mu = jnp.mean(x, -1, keepdims=True)
    xc = x - mu
    var = jnp.mean(xc * xc, -1, keepdims=True)
    return xc * lax.rsqrt(var + LN_EPS) * g + b


def _s5_param_kernel(lre_ref, lim_ref, ldt_ref, bre_ref, bim_ref, lbre_ref, lbim_ref, bbre_ref, bbim_ref):
    lam_re = lre_ref[...]
    lam_im = lim_ref[...]
    dt = jnp.exp(ldt_ref[...])
    mag = jnp.exp(lam_re * dt)
    ang = lam_im * dt
    lb_re = mag * jnp.cos(ang)
    lb_im = mag * jnp.sin(ang)
    den = lam_re * lam_re + lam_im * lam_im
    nr = lb_re - 1.0
    ni = lb_im
    f_re = (nr * lam_re + ni * lam_im) / den
    f_im = (ni * lam_re - nr * lam_im) / den
    b_re = bre_ref[...]
    b_im = bim_ref[...]
    lbre_ref[...] = lb_re
    lbim_ref[...] = lb_im
    bbre_ref[...] = f_re * b_re - f_im * b_im
    bbim_ref[...] = f_re * b_im + f_im * b_re


def _s5_params(lam_re, lam_im, log_dt, b_re, b_im):
    row = lambda a: a.reshape(1, S5_LANES)
    ldt = jnp.repeat(log_dt, S5_STATE).reshape(1, S5_LANES)
    bt = lambda b: jnp.transpose(b, (2, 0, 1)).reshape(S5_GROUP, S5_LANES)
    sds = jax.ShapeDtypeStruct
    lb_re, lb_im, bb_re, bb_im = pl.pallas_call(
        _s5_param_kernel,
        out_shape=(sds((1, S5_LANES), F32), sds((1, S5_LANES), F32),
                   sds((S5_GROUP, S5_LANES), F32), sds((S5_GROUP, S5_LANES), F32)),
    )(row(lam_re), row(lam_im), ldt, bt(b_re), bt(b_im))
    eye = jnp.eye(S5_GROUPS, dtype=F32)

    def block_diag(bb):
        bb = bb.reshape(S5_GROUP, S5_GROUPS, S5_STATE)
        return (eye[:, None, :, None] * bb[None]).reshape(S5_CH, S5_LANES)

    hc, hl = S5_CH // 2, S5_LANES // 2
    quad = lambda w, kt: w[kt * hc:(kt + 1) * hc, kt * hl:(kt + 1) * hl]
    w_re, w_im = block_diag(bb_re), block_diag(bb_im)
    w_b = jnp.stack([jnp.concatenate([quad(w_re, kt), quad(w_im, kt)], axis=1) for kt in range(2)]).astype(BF16)
    return lb_re, lb_im, w_b


def _s5_c_weights(c_re, c_im):
    eye = jnp.eye(S5_GROUPS, dtype=F32)

    def block_diag(c):
        return (jnp.transpose(c, (0, 2, 1))[:, :, None, :] * eye[:, None, :, None]).reshape(S5_LANES, S5_CH)

    hc, hl = S5_CH // 2, S5_LANES // 2
    halves = lambda w: jnp.stack([w[n * hl:(n + 1) * hl, n * hc:(n + 1) * hc] for n in range(2)]).astype(BF16)
    return halves(block_diag(c_re)), halves(block_diag(c_im))


def _inproj_kernel(x_ref, cbuf_ref, wqkv_ref, wz_ref, wba_ref, wu_ref, convw_ref, alog_ref, dtb_ref,
                   q_ref, k_ref, v_ref, gz_ref, bg_ref, u_ref, cnew_ref, ext_ref, *, bb, tt):
    tm = bb * tt
    pad = SUBLANES
    t = pl.program_id(1)
    xb = x_ref[...].astype(BF16)

    @pl.when(t == 0)
    def _():
        ext_ref[:, pad - 3:pad, :] = cbuf_ref[...]

    pre = jnp.dot(xb, wqkv_ref[...], preferred_element_type=F32)
    ext_ref[:, pad:pad + tt, :] = pre.reshape(bb, tt, CONV_CH)

    for h in range(CONV_CH // LANES):
        cs = slice(h * LANES, (h + 1) * LANES)
        acc = ext_ref[:, pad:pad + tt, cs] * convw_ref[3:4, cs]
        for j in range(DN_CONV - 1):
            acc = acc + ext_ref[:, pad - 3 + j:pad - 3 + j + tt, cs] * convw_ref[j:j + 1, cs]
        y = _silu(acc).reshape(tm, LANES)
        if h < 2 * DN_HEADS:
            y = y * lax.rsqrt(jnp.sum(y * y, -1, keepdims=True) + NORM_EPS)
        if h < DN_HEADS:
            q_ref[:, cs] = y * (DN_DK ** -0.5)
        elif h < 2 * DN_HEADS:
            k_ref[:, slice(cs.start - QK_W, cs.stop - QK_W)] = y
        else:
            v_ref[:, slice(cs.start - 2 * QK_W, cs.stop - 2 * QK_W)] = y

    carry = ext_ref[:, pad + tt - 3:pad + tt, :]
    cnew_ref[...] = carry
    ext_ref[:, pad - 3:pad, :] = carry

    z = jnp.dot(xb, wz_ref[...], preferred_element_type=F32)
    gz_ref[...] = _silu(z)
    u_ref[...] = jnp.dot(xb, wu_ref[...], preferred_element_type=F32)

    ba = jnp.dot(xb, wba_ref[...], preferred_element_type=F32)
    lane = lax.broadcasted_iota(jnp.int32, ba.shape, 1)
    beta = jax.nn.sigmoid(ba)
    sp_in = ba + dtb_ref[...]
    softplus = jnp.maximum(sp_in, 0.0) + jnp.log1p(jnp.exp(-jnp.abs(sp_in)))
    g = -jnp.exp(alog_ref[...]) * softplus
    bg = jnp.where(lane < DN_HEADS, beta, jnp.where(lane < 2 * DN_HEADS, g, 0.0))
    bg_ref[...] = bg


def _inproj(x, conv_buf, wqkv, wz, wba, wu, conv_w, alog_row, dtb_row, *, t_len, bb, tt):
    m = x.shape[0]
    bsz = m // t_len
    n_t = t_len // tt
    grid = (bsz // bb, n_t)
    tok = lambda w: pl.BlockSpec((bb * tt, w), lambda b, t: (b * n_t + t, 0))
    full = lambda a: pl.BlockSpec(a.shape, lambda b, t: (0,) * a.ndim)
    sds = jax.ShapeDtypeStruct
    return pl.pallas_call(
        functools.partial(_inproj_kernel, bb=bb, tt=tt),
        grid=grid,
        in_specs=[tok(D_MODEL), pl.BlockSpec((bb, DN_CONV - 1, CONV_CH), lambda b, t: (b, 0, 0)),
                  full(wqkv), full(wz), full(wba), full(wu), full(conv_w), full(alog_row), full(dtb_row)],
        out_specs=[tok(QK_W), tok(QK_W), tok(V_W), tok(V_W), tok(LANES), tok(S5_CH),
                   pl.BlockSpec((bb, DN_CONV - 1, CONV_CH), lambda b, t: (b, 0, 0))],
        out_shape=[sds((m, QK_W), F32), sds((m, QK_W), F32), sds((m, V_W), F32), sds((m, V_W), F32),
                   sds((m, LANES), F32), sds((m, S5_CH), F32), sds((bsz, DN_CONV - 1, CONV_CH), F32)],
        scratch_shapes=[pltpu.VMEM((bb, SUBLANES + tt, CONV_CH), F32)],
        compiler_params=_params("parallel", "arbitrary"),
        name="inproj",
    )(x, conv_buf, wqkv, wz, wba, wu, conv_w, alog_row, dtb_row)


def _delta_intra(q, k, v, bgv, head, chunk):
    rows = DELTA_ROWS
    lane = lax.broadcasted_iota(jnp.int32, (rows, LANES), 1)
    beta = jnp.sum(jnp.where(lane == head, bgv, 0.0), -1, keepdims=True)
    g = jnp.sum(jnp.where(lane == head + DN_HEADS, bgv, 0.0), -1, keepdims=True)

    ri = lax.broadcasted_iota(jnp.int32, (rows, rows), 0)
    ci = lax.broadcasted_iota(jnp.int32, (rows, rows), 1)
    shift = chunk.bit_length() - 1
    same = (ri >> shift) == (ci >> shift)
    incl = same & (ri >= ci)
    strict = same & (ri > ci)

    gb = jnp.broadcast_to(g, (rows, LANES))
    sel = jnp.concatenate([jnp.where(incl, 1.0, 0.0), jnp.where(same, 1.0, 0.0)], axis=0).astype(BF16)
    g1, g2, g3 = _split3(gb)
    cs = (jnp.dot(sel, g1, preferred_element_type=F32) + jnp.dot(sel, g2, preferred_element_type=F32)
          + jnp.dot(sel, g3, preferred_element_type=F32))
    gc = cs[:rows]
    g_last = cs[rows:]
    egc = jnp.exp(gc)

    gc_sq = jnp.concatenate([gc, gc], axis=1)
    diff = gc_sq - gc_sq.T
    decay = jnp.where(incl, jnp.exp(jnp.where(incl, diff, 0.0)), 0.0)

    kb = k * beta
    a_mat = jnp.where(strict, _mm_nt(kb, k) * decay, 0.0)
    x_mat = -a_mat
    p_mat = a_mat
    for _ in range(shift - 1):
        p_mat = _mm(p_mat, p_mat)
        x_mat = x_mat + p_mat + _mm(x_mat, p_mat)
    rhs = jnp.concatenate([v * beta, kb * egc], axis=1)
    uw = rhs + _mm(x_mat, rhs)
    u = uw[:, :DN_DV]
    w = uw[:, DN_DV:]
    qk = jnp.where(incl, _mm_nt(q, k) * decay, 0.0)
    q_dec = q * egc
    k_dec = k * jnp.exp(g_last - gc)
    return u, w, qk, q_dec, k_dec, jnp.exp(g_last)


def _delta_finish(o, gz, nw_ref, og_ref):
    o = o * lax.rsqrt(jnp.mean(o * o, -1, keepdims=True) + NORM_EPS) * nw_ref[...]
    og_ref[...] = (o * gz).astype(BF16)


def _delta_prompt_kernel(q_ref, k_ref, v_ref, gz_ref, bg_ref, s0_ref, nw_ref, og_ref, sout_ref, s_ref):
    rows, c, sb = DELTA_ROWS, DN_CHUNK, DELTA_SUB
    n_sub, cps = rows // sb, sb // c
    shift = c.bit_length() - 1
    t = pl.program_id(1)

    @pl.when(t == 0)
    def _():
        s_ref[...] = s0_ref[0]

    bgv = bg_ref[...]
    ri = lax.broadcasted_iota(jnp.int32, (rows, rows), 0)
    ci = lax.broadcasted_iota(jnp.int32, (rows, rows), 1)
    same = (ri >> shift) == (ci >> shift)
    sel = jnp.concatenate([jnp.where(same & (ri >= ci), 1.0, 0.0), jnp.where(same, 1.0, 0.0)], axis=0).astype(BF16)
    g1, g2, g3 = _split3(bgv)
    cs = (jnp.dot(sel, g1, preferred_element_type=F32) + jnp.dot(sel, g2, preferred_element_type=F32)
          + jnp.dot(sel, g3, preferred_element_type=F32))
    gcs = cs[:rows]
    gls = cs[rows:]
    gcs_t = gcs.T

    rs = lax.broadcasted_iota(jnp.int32, (sb, sb), 0)
    cc = lax.broadcasted_iota(jnp.int32, (sb, sb), 1)
    same_s = (rs >> shift) == (cc >> shift)
    incl = same_s & (rs >= cc)
    strict = same_s & (rs > cc)

    probs = [(h, s) for s in range(n_sub) for h in range(DN_HEADS)]
    bcast = lambda col: jnp.broadcast_to(col, (sb, LANES))
    q, k, kbeta, decay, rhs, q_dec, kd_t, g_tot = {}, {}, {}, {}, {}, {}, {}, {}
    for p in probs:
        h, s = p
        rsl = slice(s * sb, (s + 1) * sb)
        hs = slice(h * LANES, (h + 1) * LANES)
        lane_g = DN_HEADS + h
        beta = bcast(bgv[rsl, h:h + 1])
        gc = bcast(gcs[rsl, lane_g:lane_g + 1])
        gl = bcast(gls[rsl, lane_g:lane_g + 1])
        egc = jnp.exp(gc)
        q[p] = q_ref[rsl, hs]
        k[p] = k_ref[rsl, hs]
        kbeta[p] = k[p] * beta
        decay[p] = jnp.where(incl, jnp.exp(jnp.where(incl, gc - gcs_t[lane_g:lane_g + 1, rsl], 0.0)), 0.0)
        rhs[p] = jnp.concatenate([v_ref[rsl, hs] * beta, kbeta[p] * egc], axis=1)
        q_dec[p] = q[p] * egc
        kd_t[p] = (k[p] * jnp.exp(gl - gc)).T
        g_tot[p] = jnp.exp(gl)
    a_mat = {p: jnp.where(strict, _mm_nt(kbeta[p], k[p]) * decay[p], 0.0) for p in probs}
    qk = {p: jnp.where(incl, _mm_nt(q[p], k[p]) * decay[p], 0.0) for p in probs}
    x_mat = {p: -a_mat[p] for p in probs}
    p_mat = a_mat
    for _ in range(shift - 1):
        p_mat = {p: _mm(p_mat[p], p_mat[p]) for p in probs}
        xp = {p: _mm(x_mat[p], p_mat[p]) for p in probs}
        x_mat = {p: x_mat[p] + p_mat[p] + xp[p] for p in probs}
    uw = {p: rhs[p] + _mm(x_mat[p], rhs[p]) for p in probs}

    col = lax.broadcasted_iota(jnp.int32, (DN_DK, sb), 1)
    zeros = jnp.zeros((c, DN_DV), F32)
    state = [s_ref[h] for h in range(DN_HEADS)]
    vn_parts = {p: [] for p in probs}
    qs_parts = {p: [] for p in probs}
    for j in range(rows // c):
        s, jj = divmod(j, cps)
        sl = slice(jj * c, (jj + 1) * c)
        r = {h: _mm(jnp.concatenate([uw[(h, s)][sl, DN_DV:], q_dec[(h, s)][sl]], axis=0), state[h])
             for h in range(DN_HEADS)}
        for h in range(DN_HEADS):
            p = (h, s)
            vn = uw[p][sl, :DN_DV] - r[h][:c]
            vn_parts[p].append(vn)
            qs_parts[p].append(r[h][c:])
            vn_pad = jnp.concatenate([vn if i == jj else zeros for i in range(cps)], axis=0)
            kd_j = jnp.where((col >= jj * c) & (col < (jj + 1) * c), kd_t[p], 0.0)
            state[h] = state[h] * g_tot[p][jj * c:jj * c + 1, :] + _mm(kd_j, vn_pad)
    for h in range(DN_HEADS):
        s_ref[h] = state[h]
    for p in probs:
        h, s = p
        o = jnp.concatenate(qs_parts[p], axis=0) + _mm(qk[p], jnp.concatenate(vn_parts[p], axis=0))
        o = o * lax.rsqrt(jnp.mean(o * o, -1, keepdims=True) + NORM_EPS) * nw_ref[...]
        rsl = slice(s * sb, (s + 1) * sb)
        hs = slice(h * LANES, (h + 1) * LANES)
        og_ref[rsl, hs] = (o * gz_ref[rsl, hs]).astype(BF16)

    @pl.when(t == pl.num_programs(1) - 1)
    def _():
        sout_ref[0] = s_ref[...]


def _delta_sample_kernel(q_ref, k_ref, v_ref, gz_ref, bg_ref, s0_ref, nw_ref, og_ref, sout_ref, *, tt):
    rows, c = DELTA_ROWS, tt
    nseq = rows // c
    head = pl.program_id(1)
    u, w, qk, q_dec, k_dec, g_tot = _delta_intra(q_ref[...], k_ref[...], v_ref[...], bg_ref[...], head, c)
    kd_t = k_dec.T
    col = lax.broadcasted_iota(jnp.int32, kd_t.shape, 1)
    row = lax.broadcasted_iota(jnp.int32, (rows, DN_DV), 0)
    vn_parts, qs_parts = [], []
    for i in range(nseq):
        sl = slice(i * c, (i + 1) * c)
        s = s0_ref[i, 0]
        r = _mm(jnp.concatenate([w[sl], q_dec[sl]], axis=0), s)
        vn_parts.append(u[sl] - r[:c])
        qs_parts.append(r[c:])
    vn_all = jnp.concatenate(vn_parts, axis=0)
    for i in range(nseq):
        kd_i = jnp.where((col >= i * c) & (col < (i + 1) * c), kd_t, 0.0)
        sout_ref[i, 0] = s0_ref[i, 0] * g_tot[i * c:i * c + 1, :] + _mm(kd_i, vn_all)
    o = jnp.concatenate(qs_parts, axis=0) + _mm(qk, vn_all)
    _delta_finish(o, gz_ref[...], nw_ref, og_ref)


def _delta(q, k, v, gz, bg, s0, norm_w, *, t_len):
    m = q.shape[0]
    bsz = m // t_len
    sds = jax.ShapeDtypeStruct
    out_shape = [sds((m, V_W), BF16), sds((bsz, DN_HEADS, DN_DK, DN_DV), F32)]
    if t_len % DELTA_ROWS == 0:
        n_t = t_len // DELTA_ROWS
        grid = (bsz, n_t)
        tok = lambda w: pl.BlockSpec((DELTA_ROWS, w), lambda b, t: (b * n_t + t, 0))
        st = pl.BlockSpec((1, DN_HEADS, DN_DK, DN_DV), lambda b, t: (b, 0, 0, 0))
        return pl.pallas_call(
            _delta_prompt_kernel, grid=grid,
            in_specs=[tok(QK_W), tok(QK_W), tok(V_W), tok(V_W), tok(LANES), st,
                      pl.BlockSpec((1, LANES), lambda b, t: (0, 0))],
            out_specs=[tok(V_W), st], out_shape=out_shape,
            scratch_shapes=[pltpu.VMEM((DN_HEADS, DN_DK, DN_DV), F32)],
            compiler_params=_params("parallel", "arbitrary"),
            name="delta_prompt",
        )(q, k, v, gz, bg, s0, norm_w)
    assert t_len <= DN_CHUNK and DELTA_ROWS % t_len == 0 and t_len % SUBLANES == 0
    nseq = DELTA_ROWS // t_len
    assert bsz % nseq == 0
    grid = (bsz // nseq, DN_HEADS)
    hd = pl.BlockSpec((DELTA_ROWS, LANES), lambda b, h: (b, h))
    st = pl.BlockSpec((nseq, 1, DN_DK, DN_DV), lambda b, h: (b, h, 0, 0))
    return pl.pallas_call(
        functools.partial(_delta_sample_kernel, tt=t_len), grid=grid,
        in_specs=[hd, hd, hd, hd, pl.BlockSpec((DELTA_ROWS, LANES), lambda b, h: (b, 0)), st,
                  pl.BlockSpec((1, LANES), lambda b, h: (0, 0))],
        out_specs=[hd, st], out_shape=out_shape,
        compiler_params=_params("parallel", "parallel"),
        name="delta_sample",
    )(q, k, v, gz, bg, s0, norm_w)


def _s5_perm(bb, tt):
    i = np.arange(bb * tt)
    gi, r = np.divmod(i, SUBLANES * tt)
    ti, bl = np.divmod(r, SUBLANES)
    perm = np.zeros((bb * tt, bb * tt), np.float32)
    perm[i, gi * SUBLANES * tt + bl * tt + ti] = 1.0
    return perm


def _s5_kernel(u_ref, h0re_ref, h0im_ref, perm_ref, permt_ref, wb_ref, lbre_ref, lbim_ref, wcre_ref, wcim_ref, d_ref,
               yb_ref, hre_ref, him_ref, hbuf_ref, hst_ref, *, bb, tt):
    rows = bb * tt
    t = pl.program_id(1)
    n_tiles = S5_LANES // LANES
    perm = perm_ref[...]
    u2 = sum(jnp.dot(perm, piece, preferred_element_type=F32) for piece in _split3(u_ref[...].reshape(rows, S5_CH)))
    u2b = u2.astype(BF16)
    half_t = n_tiles // 2
    for kt in range(2):
        bu = jnp.dot(u2b[:, kt * (S5_CH // 2):(kt + 1) * (S5_CH // 2)], wb_ref[kt], preferred_element_type=F32)
        for j in range(half_t):
            hbuf_ref[kt * half_t + j] = bu[:, j * LANES:(j + 1) * LANES]
            hbuf_ref[n_tiles + kt * half_t + j] = bu[:, (half_t + j) * LANES:(half_t + j + 1) * LANES]

    @pl.when(t == 0)
    def _():
        hst_ref[:, :, :S5_LANES] = h0re_ref[...].reshape(bb // SUBLANES, SUBLANES, S5_LANES)
        hst_ref[:, :, S5_LANES:] = h0im_ref[...].reshape(bb // SUBLANES, SUBLANES, S5_LANES)

    blk = 8
    for gi in range(bb // SUBLANES):
        for c0 in range(0, n_tiles, blk):
            lanes = [slice((c0 + j) * LANES, (c0 + j + 1) * LANES) for j in range(blk)]
            a_re = [lbre_ref[:, s] for s in lanes]
            a_im = [lbim_ref[:, s] for s in lanes]

            def step(i, carry):
                idx = pl.ds(pl.multiple_of((gi * tt + i) * SUBLANES, SUBLANES), SUBLANES)
                new = []
                for j in range(blk):
                    h_re, h_im = carry[j]
                    n_re = a_re[j] * h_re - a_im[j] * h_im + hbuf_ref[c0 + j, idx, :]
                    n_im = a_re[j] * h_im + a_im[j] * h_re + hbuf_ref[n_tiles + c0 + j, idx, :]
                    hbuf_ref[c0 + j, idx, :] = n_re
                    hbuf_ref[n_tiles + c0 + j, idx, :] = n_im
                    new.append((n_re, n_im))
                return tuple(new)

            init = tuple((hst_ref[gi, :, s], hst_ref[gi, :, slice(S5_LANES + s.start, S5_LANES + s.stop)])
                         for s in lanes)
            fin = lax.fori_loop(0, tt, step, init)
            for j, s in enumerate(lanes):
                hst_ref[gi, :, s] = fin[j][0]
                hst_ref[gi, :, slice(S5_LANES + s.start, S5_LANES + s.stop)] = fin[j][1]

    y_halves = []
    for n in range(2):
        h_re = jnp.concatenate([hbuf_ref[n * half_t + c] for c in range(half_t)], axis=1)
        h_im = jnp.concatenate([hbuf_ref[n_tiles + n * half_t + c] for c in range(half_t)], axis=1)
        y_halves.append(jnp.dot(h_re.astype(BF16), wcre_ref[n], preferred_element_type=F32)
                        - jnp.dot(h_im.astype(BF16), wcim_ref[n], preferred_element_type=F32))
    y = jnp.concatenate(y_halves, axis=1) + d_ref[...] * u2
    yb = jnp.dot(permt_ref[...], jax.nn.gelu(y).astype(BF16), preferred_element_type=F32)
    yb_ref[...] = yb.reshape(yb_ref.shape).astype(BF16)
    hre_ref[...] = hst_ref[:, :, :S5_LANES].reshape(bb, S5_LANES)
    him_ref[...] = hst_ref[:, :, S5_LANES:].reshape(bb, S5_LANES)


def _s5(u, h0_re, h0_im, w_b, lb_re, lb_im, wc_re, wc_im, d_row, *, t_len, bb, tt):
    m = u.shape[0]
    bsz = m // t_len
    grid = (bsz // bb, t_len // tt)
    if tt == t_len:
        view = (1, m, S5_CH)
        tok = pl.BlockSpec((1, bb * tt, S5_CH), lambda b, t: (0, b, 0))
    else:
        view = (bsz, t_len, S5_CH)
        tok = pl.BlockSpec((bb, tt, S5_CH), lambda b, t: (b, t, 0))
    full = lambda a: pl.BlockSpec(a.shape, lambda b, t: (0,) * a.ndim)
    st = pl.BlockSpec((bb, S5_LANES), lambda b, t: (b, 0))
    sds = jax.ShapeDtypeStruct
    perm = _s5_perm(bb, tt)
    perm, perm_t = jnp.asarray(perm, BF16), jnp.asarray(perm.T, BF16)
    yb, h_re, h_im = pl.pallas_call(
        functools.partial(_s5_kernel, bb=bb, tt=tt), grid=grid,
        in_specs=[tok, st, st, full(perm), full(perm_t), full(w_b), full(lb_re), full(lb_im), full(wc_re),
                  full(wc_im), full(d_row)],
        out_specs=[tok, st, st],
        out_shape=[sds(view, BF16), sds((bsz, S5_LANES), F32), sds((bsz, S5_LANES), F32)],
        scratch_shapes=[pltpu.VMEM((2 * S5_LANES // LANES, bb * tt, LANES), F32),
                        pltpu.VMEM((bb // SUBLANES, SUBLANES, 2 * S5_LANES), F32)],
        compiler_params=_params("parallel", "arbitrary"),
        name="s5",
    )(u.reshape(view), h0_re, h0_im, perm, perm_t, w_b, lb_re, lb_im, wc_re, wc_im, d_row)
    return yb.reshape(m, S5_CH), h_re, h_im


def _merge_kernel(x_ref, og_ref, yb_ref, wdn_ref, wglu_ref, wgab_ref, wout_ref, g_ref, b_ref, o_ref, *, alpha):
    x = x_ref[...]
    branch_a = jnp.dot(og_ref[...], wdn_ref[...], preferred_element_type=F32)
    glu = jnp.dot(yb_ref[...], wglu_ref[...], preferred_element_type=F32)
    branch_b = glu[:, :D_MODEL] * jax.nn.sigmoid(glu[:, D_MODEL:])
    gab = jnp.dot(x.astype(BF16), wgab_ref[...], preferred_element_type=F32)
    mix_in = jax.nn.sigmoid(gab[:, :D_MODEL]) * branch_a + jax.nn.sigmoid(gab[:, D_MODEL:]) * branch_b
    mix = jnp.dot(mix_in.astype(BF16), wout_ref[...], preferred_element_type=F32)
    o_ref[...] = _layer_norm(alpha * x + mix, g_ref[...], b_ref[...])


def _merge(x, og, yb, wdn, wglu, wgab, wout, ln_g, ln_b, *, alpha):
    m = x.shape[0]
    tm = ROW_TILE
    row = lambda w: pl.BlockSpec((tm, w), lambda i: (i, 0))
    full = lambda a: pl.BlockSpec(a.shape, lambda i: (0,) * a.ndim)
    return pl.pallas_call(
        functools.partial(_merge_kernel, alpha=alpha), grid=(m // tm,),
        in_specs=[row(D_MODEL), row(V_W), row(S5_CH), full(wdn), full(wglu), full(wgab), full(wout),
                  full(ln_g), full(ln_b)],
        out_specs=row(D_MODEL), out_shape=jax.ShapeDtypeStruct((m, D_MODEL), F32),
        compiler_params=_params("parallel"),
        name="merge",
    )(x, og, yb, wdn, wglu, wgab, wout, ln_g, ln_b)


def _route(x, wr_hi_ref, wr_lo_ref, br_ref):
    x_hi, x_lo = _split2(x)
    logits = (jnp.dot(x_hi, wr_hi_ref[...], preferred_element_type=F32)
              + jnp.dot(x_hi, wr_lo_ref[...], preferred_element_type=F32)
              + jnp.dot(x_lo, wr_hi_ref[...], preferred_element_type=F32)) + br_ref[...]
    lane = lax.broadcasted_iota(jnp.int32, logits.shape, 1).astype(F32)
    first = lambda mask: jnp.min(jnp.where(mask, lane, float(LANES)), -1, keepdims=True)
    lg = jnp.where(lane < MOE_GROUPS, logits, NEG_BIG)
    g_max = jnp.max(lg, -1, keepdims=True)
    g_idx = first(lg == g_max)
    g_w = 1.0 / jnp.sum(jnp.where(lane < MOE_GROUPS, jnp.exp(lg - g_max), 0.0), -1, keepdims=True)
    lo = MOE_GROUPS + MOE_PER_GROUP * g_idx
    le = jnp.where((lane >= lo) & (lane < lo + MOE_PER_GROUP), logits, NEG_BIG)
    m1 = jnp.max(le, -1, keepdims=True)
    i1 = first(le == m1)
    le2 = jnp.where(lane == i1, NEG_BIG, le)
    m2 = jnp.max(le2, -1, keepdims=True)
    i2 = first(le2 == m2)
    e2 = jnp.exp(m2 - m1)
    w1 = g_w / (1.0 + e2)
    w2 = g_w * e2 / (1.0 + e2)
    return jnp.where(lane == i1, w1, 0.0) + jnp.where(lane == i2, w2, 0.0)


def _moe_kernel(x_ref, p_ref, wrh_ref, wrl_ref, br_ref, wgu_ref, wd_ref, g2_ref, b2_ref, wpg_ref, wple_ref,
                g3_ref, b3_ref, o_ref, acc_ref, comb_ref, xb_ref, *, alpha):
    e = pl.program_id(1)

    @pl.when(e == 0)
    def _():
        x = x_ref[...]
        comb_ref[...] = _route(x, wrh_ref, wrl_ref, br_ref)
        xb_ref[...] = x.astype(BF16)
        acc_ref[...] = jnp.zeros_like(acc_ref)

    xb = xb_ref[...]
    comb = comb_ref[...]
    lane = lax.broadcasted_iota(jnp.int32, comb.shape, 1)
    gu = [jnp.dot(xb, wgu_ref[i], preferred_element_type=F32) for i in range(MOE_PER_STEP)]
    out = None
    for i in range(MOE_PER_STEP):
        h = _silu(gu[i][:, :MOE_HIDDEN]) * gu[i][:, MOE_HIDDEN:]
        c_e = jnp.sum(jnp.where(lane == e * MOE_PER_STEP + i + MOE_GROUPS, comb, 0.0), -1, keepdims=True)
        d = jnp.dot((h * c_e).astype(BF16), wd_ref[i], preferred_element_type=F32)
        out = d if out is None else out + d
    acc_ref[...] += out

    @pl.when(e == MOE_EXPERTS // MOE_PER_STEP - 1)
    def _():
        x2 = _layer_norm(alpha * x_ref[...] + acc_ref[...], g2_ref[...], b2_ref[...])
        gate = jax.nn.sigmoid(jnp.dot(x2.astype(BF16), wpg_ref[...], preferred_element_type=F32))
        ple = gate * jnp.dot(p_ref[...].astype(BF16), wple_ref[...], preferred_element_type=F32)
        o_ref[...] = _layer_norm(alpha * x2 + ple, g3_ref[...], b3_ref[...])


def _moe(x, p, wr_hi, wr_lo, b_r, wgu, wd, g2, b2, wpg, wple, g3, b3, *, alpha):
    m = x.shape[0]
    tm = MOE_TILE
    row = lambda w: pl.BlockSpec((tm, w), lambda i, e: (i, 0))
    full = lambda a: pl.BlockSpec(a.shape, lambda i, e: (0,) * a.ndim)
    return pl.pallas_call(
        functools.partial(_moe_kernel, alpha=alpha), grid=(m // tm, MOE_EXPERTS // MOE_PER_STEP),
        in_specs=[row(D_MODEL), row(PLE_DIM), full(wr_hi), full(wr_lo), full(b_r),
                  pl.BlockSpec((MOE_PER_STEP, D_MODEL, 2 * MOE_HIDDEN), lambda i, e: (e, 0, 0)),
                  pl.BlockSpec((MOE_PER_STEP, MOE_HIDDEN, D_MODEL), lambda i, e: (e, 0, 0)),
                  full(g2), full(b2), full(wpg), full(wple), full(g3), full(b3)],
        out_specs=row(D_MODEL), out_shape=jax.ShapeDtypeStruct((m, D_MODEL), F32),
        scratch_shapes=[pltpu.VMEM((tm, D_MODEL), F32), pltpu.VMEM((tm, LANES), F32), pltpu.VMEM((tm, D_MODEL), BF16)],
        compiler_params=_params("parallel", "arbitrary"),
        name="moe",
    )(x, p, wr_hi, wr_lo, b_r, wgu, wd, g2, b2, wpg, wple, g3, b3)


def _prep_weights(i, w_in, conv_w, dn_a_log, dn_dt_bias, dn_norm_w, w_dn_out,
                  s5_lam_re, s5_lam_im, s5_log_dt, s5_b_re, s5_b_im, s5_c_re, s5_c_im, s5_d, w_glu,
                  w_out, ln1_g, ln1_b, w_rg, b_rg, w_re, b_re, w_gate, w_up, w_down, ln2_g, ln2_b,
                  w_ple, w_ple_gate, ln3_g, ln3_b):
    o_z = CONV_CH
    o_b = o_z + V_W
    o_u = o_b + 2 * DN_HEADS
    o_ga = o_u + S5_CH
    wi = w_in[i]
    row = lambda a, width=None: a.reshape(1, -1)
    head_pad = lambda a, off: jnp.zeros((1, LANES), F32).at[0, off:off + DN_HEADS].set(a)
    lb_re, lb_im, w_b = _s5_params(s5_lam_re[i], s5_lam_im[i], s5_log_dt[i], s5_b_re[i], s5_b_im[i])
    wc_re, wc_im = _s5_c_weights(s5_c_re[i], s5_c_im[i])
    w_router = jnp.zeros((D_MODEL, LANES), F32)
    w_router = w_router.at[:, :MOE_GROUPS].set(w_rg[i]).at[:, MOE_GROUPS:MOE_GROUPS + MOE_EXPERTS].set(w_re[i])
    b_router = jnp.zeros((1, LANES), F32)
    b_router = b_router.at[0, :MOE_GROUPS].set(b_rg[i]).at[0, MOE_GROUPS:MOE_GROUPS + MOE_EXPERTS].set(b_re[i])
    wr_hi = w_router.astype(BF16)
    wr_lo = (w_router - wr_hi.astype(F32)).astype(BF16)
    return dict(
        wqkv=wi[:, :CONV_CH].astype(BF16), wz=wi[:, o_z:o_b].astype(BF16),
        wba=jnp.pad(wi[:, o_b:o_u], ((0, 0), (0, LANES - 2 * DN_HEADS))).astype(BF16),
        wu=wi[:, o_u:o_ga].astype(BF16), wgab=wi[:, o_ga:].astype(BF16),
        conv_w=conv_w[i], alog=head_pad(dn_a_log[i], DN_HEADS), dtb=head_pad(dn_dt_bias[i], DN_HEADS),
        norm_w=row(dn_norm_w[i]), wdn=w_dn_out[i].astype(BF16),
        lb_re=lb_re, lb_im=lb_im, w_b=w_b, wc_re=wc_re, wc_im=wc_im, s5_d=row(s5_d[i]),
        wglu=w_glu[i].astype(BF16), wout=w_out[i].astype(BF16), ln1_g=row(ln1_g[i]), ln1_b=row(ln1_b[i]),
        wr_hi=wr_hi, wr_lo=wr_lo, b_r=b_router,
        wgu=jnp.concatenate([w_gate[i], w_up[i]], axis=-1).astype(BF16), wd=w_down[i].astype(BF16),
        ln2_g=row(ln2_g[i]), ln2_b=row(ln2_b[i]), wpg=w_ple_gate[i].astype(BF16), wple=w_ple[i].astype(BF16),
        ln3_g=row(ln3_g[i]), ln3_b=row(ln3_b[i]),
    )


def _layer(x, p, conv_buf, s_delta, h_re, h_im, w, alpha):
    bsz, t_len, _ = x.shape
    m = bsz * t_len
    assert t_len >= DN_CONV - 1
    if t_len % ROW_TILE == 0:
        bb, tt = 1, ROW_TILE
        s5_bb, s5_tt = SUBLANES, S5_ROWS // SUBLANES
    else:
        assert ROW_TILE % t_len == 0 and S5_ROWS % t_len == 0
        bb, tt = ROW_TILE // t_len, t_len
        s5_bb, s5_tt = S5_ROWS // t_len, t_len
    x2d = x.reshape(m, D_MODEL)
    q, k, v, gz, bg, u, conv_new = _inproj(x2d, conv_buf, w["wqkv"], w["wz"], w["wba"], w["wu"], w["conv_w"],
                                           w["alog"], w["dtb"], t_len=t_len, bb=bb, tt=tt)
    og, s_new = _delta(q, k, v, gz, bg, s_delta, w["norm_w"], t_len=t_len)
    yb, hr_new, hi_new = _s5(u, h_re.reshape(bsz, S5_LANES), h_im.reshape(bsz, S5_LANES), w["w_b"], w["lb_re"],
                             w["lb_im"], w["wc_re"], w["wc_im"], w["s5_d"], t_len=t_len, bb=s5_bb, tt=s5_tt)
    x1 = _merge(x2d, og, yb, w["wdn"], w["wglu"], w["wgab"], w["wout"], w["ln1_g"], w["ln1_b"], alpha=alpha)
    x3 = _moe(x1, p.reshape(m, PLE_DIM), w["wr_hi"], w["wr_lo"], w["b_r"], w["wgu"], w["wd"], w["ln2_g"],
              w["ln2_b"], w["wpg"], w["wple"], w["ln3_g"], w["ln3_b"], alpha=alpha)
    return (x3.reshape(bsz, t_len, D_MODEL), s_new, conv_new,
            hr_new.reshape(bsz, S5_GROUPS, S5_STATE), hi_new.reshape(bsz, S5_GROUPS, S5_STATE))


def kernel(x_prompt, x_sample, state_delta, state_conv, state_ssm_re, state_ssm_im, p_prompt, p_sample, w_in, conv_w, dn_a_log, dn_dt_bias, dn_norm_w, w_dn_out, s5_lam_re, s5_lam_im, s5_log_dt, s5_b_re, s5_b_im, s5_c_re, s5_c_im, s5_d, w_glu, w_out, ln1_g, ln1_b, w_rg, b_rg, w_re, b_re, w_gate, w_up, w_down, ln2_g, ln2_b, w_ple, w_ple_gate, ln3_g, ln3_b):
    weights = (w_in, conv_w, dn_a_log, dn_dt_bias, dn_norm_w, w_dn_out, s5_lam_re, s5_lam_im, s5_log_dt, s5_b_re,
               s5_b_im, s5_c_re, s5_c_im, s5_d, w_glu, w_out, ln1_g, ln1_b, w_rg, b_rg, w_re, b_re, w_gate, w_up,
               w_down, ln2_g, ln2_b, w_ple, w_ple_gate, ln3_g, ln3_b)
    depth = w_in.shape[0]
    alpha = (2 * depth) ** 0.25
    bp = x_prompt.shape[0]
    y_p, y_s = x_prompt, x_sample
    outs_p, outs_s = [], []
    for i in range(depth):
        w = _prep_weights(i, *weights)
        y_p, *st_p = _layer(y_p, p_prompt[i], jnp.zeros((bp, DN_CONV - 1, CONV_CH), F32),
                            jnp.zeros((bp, DN_HEADS, DN_DK, DN_DV), F32),
                            jnp.zeros((bp, S5_GROUPS, S5_STATE), F32), jnp.zeros((bp, S5_GROUPS, S5_STATE), F32),
                            w, alpha)
        y_s, *st_s = _layer(y_s, p_sample[i], state_conv[i], state_delta[i], state_ssm_re[i], state_ssm_im[i],
                            w, alpha)
        outs_p.append(st_p)
        outs_s.append(st_s)
    stack = lambda outs, j: jnp.stack([o[j] for o in outs])
    return (y_p, y_s, stack(outs_p, 0), stack(outs_p, 1), stack(outs_p, 2), stack(outs_p, 3),
            stack(outs_s, 0), stack(outs_s, 1), stack(outs_s, 2), stack(outs_s, 3))
```

```python
import functools

import jax
import jax.numpy as jnp
import numpy as np
from jax import lax
from jax.experimental import pallas as pl
from jax.experimental.pallas import tpu as pltpu

F32 = jnp.float32
BF16 = jnp.bfloat16

D_MODEL = 1024
DN_HEADS = 8
DN_DK = 128
DN_DV = 128
DN_CONV = 4
DN_CHUNK = 64
QK_W = DN_HEADS * DN_DK
V_W = DN_HEADS * DN_DV
CONV_CH = 2 * QK_W + V_W
S5_CH = 512
S5_GROUP = 16
S5_GROUPS = S5_CH // S5_GROUP
S5_STATE = 64
S5_LANES = S5_GROUPS * S5_STATE
MOE_GROUPS = 4
MOE_PER_GROUP = 4
MOE_EXPERTS = MOE_GROUPS * MOE_PER_GROUP
MOE_HIDDEN = 256
PLE_DIM = 256
LN_EPS = 1e-5
NORM_EPS = 1e-6

LANES = 128
SUBLANES = 8
VMEM_LIMIT_BYTES = 56 * 1024 * 1024

ROW_TILE = 256
MERGE_TILE = 512
MOE_TILE = 512
MOE_PER_STEP = 16
S5_ROWS = 512
DELTA_ROWS = 256
DELTA_SUB = 128
NEG_BIG = -1e30


def _params(*sem):
    return pltpu.CompilerParams(dimension_semantics=sem, vmem_limit_bytes=VMEM_LIMIT_BYTES)


def _mm(a, b):
    return jnp.dot(a.astype(BF16), b.astype(BF16), preferred_element_type=F32)


def _mm_nt(a, b):
    return lax.dot_general(a.astype(BF16), b.astype(BF16), (((1,), (1,)), ((), ())),
                           preferred_element_type=F32)


def _split3(x):
    x1 = x.astype(BF16)
    r1 = x - x1.astype(F32)
    x2 = r1.astype(BF16)
    r2 = r1 - x2.astype(F32)
    return x1, x2, r2.astype(BF16)


def _split2(x):
    x1 = x.astype(BF16)
    return x1, (x - x1.astype(F32)).astype(BF16)


def _silu(x):
    return x * jax.nn.sigmoid(x)


def _layer_norm(x, g, b):
    mu = jnp.mean(x, -1, keepdims=True)
    xc = x - mu
    var = jnp.mean(xc * xc, -1, keepdims=True)
    return xc * lax.rsqrt(var + LN_EPS) * g + b


def _s5_param_kernel(lre_ref, lim_ref, ldt_ref, bre_ref, bim_ref, lbre_ref, lbim_ref, bbre_ref, bbim_ref):
    lam_re = lre_ref[...]
    lam_im = lim_ref[...]
    dt = jnp.exp(ldt_ref[...])
    mag = jnp.exp(lam_re * dt)
    ang = lam_im * dt
    lb_re = mag * jnp.cos(ang)
    lb_im = mag * jnp.sin(ang)
    den = lam_re * lam_re + lam_im * lam_im
    nr = lb_re - 1.0
    ni = lb_im
    f_re = (nr * lam_re + ni * lam_im) / den
    f_im = (ni * lam_re - nr * lam_im) / den
    b_re = bre_ref[...]
    b_im = bim_ref[...]
    lbre_ref[...] = lb_re
    lbim_ref[...] = lb_im
    bbre_ref[...] = f_re * b_re - f_im * b_im
    bbim_ref[...] = f_re * b_im + f_im * b_re


def _s5_params(lam_re, lam_im, log_dt, b_re, b_im):
    row = lambda a: a.reshape(1, S5_LANES)
    ldt = jnp.repeat(log_dt, S5_STATE).reshape(1, S5_LANES)
    bt = lambda b: jnp.transpose(b, (2, 0, 1)).reshape(S5_GROUP, S5_LANES)
    sds = jax.ShapeDtypeStruct
    lb_re, lb_im, bb_re, bb_im = pl.pallas_call(
        _s5_param_kernel,
        out_shape=(sds((1, S5_LANES), F32), sds((1, S5_LANES), F32),
                   sds((S5_GROUP, S5_LANES), F32), sds((S5_GROUP, S5_LANES), F32)),
    )(row(lam_re), row(lam_im), ldt, bt(b_re), bt(b_im))
    eye = jnp.eye(S5_GROUPS, dtype=F32)

    def block_diag(bb):
        bb = bb.reshape(S5_GROUP, S5_GROUPS, S5_STATE)
        return (eye[:, None, :, None] * bb[None]).reshape(S5_CH, S5_LANES)

    hc, hl = S5_CH // 2, S5_LANES // 2
    quad = lambda w, kt: w[kt * hc:(kt + 1) * hc, kt * hl:(kt + 1) * hl]
    w_re, w_im = block_diag(bb_re), block_diag(bb_im)
    w_b = jnp.stack([jnp.concatenate([quad(w_re, kt), quad(w_im, kt)], axis=1) for kt in range(2)]).astype(BF16)
    return lb_re, lb_im, w_b


def _s5_c_weights(c_re, c_im):
    eye = jnp.eye(S5_GROUPS, dtype=F32)

    def block_diag(c):
        return (jnp.transpose(c, (0, 2, 1))[:, :, None, :] * eye[:, None, :, None]).reshape(S5_LANES, S5_CH)

    hc, hl = S5_CH // 2, S5_LANES // 2
    halves = lambda w: jnp.stack([w[n * hl:(n + 1) * hl, n * hc:(n + 1) * hc] for n in range(2)]).astype(BF16)
    return halves(block_diag(c_re)), halves(block_diag(c_im))


def _inproj_kernel(x_ref, cbuf_ref, wqkv_ref, wz_ref, wba_ref, wu_ref, convw_ref, alog_ref, dtb_ref,
                   q_ref, k_ref, v_ref, gz_ref, bg_ref, u_ref, cnew_ref, ext_ref, *, bb, tt):
    tm = bb * tt
    pad = SUBLANES
    t = pl.program_id(1)
    xb = x_ref[...].astype(BF16)

    @pl.when(t == 0)
    def _():
        ext_ref[:, pad - 3:pad, :] = cbuf_ref[...]

    pre = jnp.dot(xb, wqkv_ref[...], preferred_element_type=F32)
    ext_ref[:, pad:pad + tt, :] = pre.reshape(bb, tt, CONV_CH)

    for h in range(CONV_CH // LANES):
        cs = slice(h * LANES, (h + 1) * LANES)
        acc = ext_ref[:, pad:pad + tt, cs] * convw_ref[3:4, cs]
        for j in range(DN_CONV - 1):
            acc = acc + ext_ref[:, pad - 3 + j:pad - 3 + j + tt, cs] * convw_ref[j:j + 1, cs]
        y = _silu(acc).reshape(tm, LANES)
        if h < 2 * DN_HEADS:
            y = y * lax.rsqrt(jnp.sum(y * y, -1, keepdims=True) + NORM_EPS)
        if h < DN_HEADS:
            q_ref[:, cs] = y * (DN_DK ** -0.5)
        elif h < 2 * DN_HEADS:
            k_ref[:, slice(cs.start - QK_W, cs.stop - QK_W)] = y
        else:
            v_ref[:, slice(cs.start - 2 * QK_W, cs.stop - 2 * QK_W)] = y

    carry = ext_ref[:, pad + tt - 3:pad + tt, :]
    cnew_ref[...] = carry
    ext_ref[:, pad - 3:pad, :] = carry

    z = jnp.dot(xb, wz_ref[...], preferred_element_type=F32)
    gz_ref[...] = _silu(z)
    u_ref[...] = jnp.dot(xb, wu_ref[...], preferred_element_type=F32)

    ba = jnp.dot(xb, wba_ref[...], preferred_element_type=F32)
    lane = lax.broadcasted_iota(jnp.int32, ba.shape, 1)
    beta = jax.nn.sigmoid(ba)
    sp_in = ba + dtb_ref[...]
    softplus = jnp.maximum(sp_in, 0.0) + jnp.log1p(jnp.exp(-jnp.abs(sp_in)))
    g = -jnp.exp(alog_ref[...]) * softplus
    bg = jnp.where(lane < DN_HEADS, beta, jnp.where(lane < 2 * DN_HEADS, g, 0.0))
    bg_ref[...] = bg


def _inproj(x, conv_buf, wqkv, wz, wba, wu, conv_w, alog_row, dtb_row, *, t_len, bb, tt):
    m = x.shape[0]
    bsz = m // t_len
    n_t = t_len // tt
    grid = (bsz // bb, n_t)
    tok = lambda w: pl.BlockSpec((bb * tt, w), lambda b, t: (b * n_t + t, 0))
    full = lambda a: pl.BlockSpec(a.shape, lambda b, t: (0,) * a.ndim, pipeline_mode=pl.Buffered(1))
    sds = jax.ShapeDtypeStruct
    return pl.pallas_call(
        functools.partial(_inproj_kernel, bb=bb, tt=tt),
        grid=grid,
        in_specs=[tok(D_MODEL), pl.BlockSpec((bb, DN_CONV - 1, CONV_CH), lambda b, t: (b, 0, 0)),
                  full(wqkv), full(wz), full(wba), full(wu), full(conv_w), full(alog_row), full(dtb_row)],
        out_specs=[tok(QK_W), tok(QK_W), tok(V_W), tok(V_W), tok(LANES), tok(S5_CH),
                   pl.BlockSpec((bb, DN_CONV - 1, CONV_CH), lambda b, t: (b, 0, 0))],
        out_shape=[sds((m, QK_W), F32), sds((m, QK_W), F32), sds((m, V_W), F32), sds((m, V_W), F32),
                   sds((m, LANES), F32), sds((m, S5_CH), F32), sds((bsz, DN_CONV - 1, CONV_CH), F32)],
        scratch_shapes=[pltpu.VMEM((bb, SUBLANES + tt, CONV_CH), F32)],
        compiler_params=_params("parallel", "arbitrary"),
        name="inproj",
    )(x, conv_buf, wqkv, wz, wba, wu, conv_w, alog_row, dtb_row)


def _delta_intra(q, k, v, bgv, head, chunk):
    rows = DELTA_ROWS
    lane = lax.broadcasted_iota(jnp.int32, (rows, LANES), 1)
    beta = jnp.sum(jnp.where(lane == head, bgv, 0.0), -1, keepdims=True)
    g = jnp.sum(jnp.where(lane == head + DN_HEADS, bgv, 0.0), -1, keepdims=True)

    ri = lax.broadcasted_iota(jnp.int32, (rows, rows), 0)
    ci = lax.broadcasted_iota(jnp.int32, (rows, rows), 1)
    shift = chunk.bit_length() - 1
    same = (ri >> shift) == (ci >> shift)
    incl = same & (ri >= ci)
    strict = same & (ri > ci)

    gb = jnp.broadcast_to(g, (rows, LANES))
    sel = jnp.concatenate([jnp.where(incl, 1.0, 0.0), jnp.where(same, 1.0, 0.0)], axis=0).astype(BF16)
    g1, g2, g3 = _split3(gb)
    cs = (jnp.dot(sel, g1, preferred_element_type=F32) + jnp.dot(sel, g2, preferred_element_type=F32)
          + jnp.dot(sel, g3, preferred_element_type=F32))
    gc = cs[:rows]
    g_last = cs[rows:]
    egc = jnp.exp(gc)

    gc_sq = jnp.concatenate([gc, gc], axis=1)
    diff = gc_sq - gc_sq.T
    decay = jnp.where(incl, jnp.exp(jnp.where(incl, diff, 0.0)), 0.0)

    kb = k * beta
    a_mat = jnp.where(strict, _mm_nt(kb, k) * decay, 0.0)
    x_mat = -a_mat
    p_mat = a_mat
    for _ in range(shift - 1):
        p_mat = _mm(p_mat, p_mat)
        x_mat = x_mat + p_mat + _mm(x_mat, p_mat)
    rhs = jnp.concatenate([v * beta, kb * egc], axis=1)
    uw = rhs + _mm(x_mat, rhs)
    u = uw[:, :DN_DV]
    w = uw[:, DN_DV:]
    qk = jnp.where(incl, _mm_nt(q, k) * decay, 0.0)
    q_dec = q * egc
    k_dec = k * jnp.exp(g_last - gc)
    return u, w, qk, q_dec, k_dec, jnp.exp(g_last)


def _delta_finish(o, gz, nw_ref, og_ref):
    o = o * lax.rsqrt(jnp.mean(o * o, -1, keepdims=True) + NORM_EPS) * nw_ref[...]
    og_ref[...] = (o * gz).astype(BF16)


def _delta_prompt_kernel(q_ref, k_ref, v_ref, gz_ref, bg_ref, s0_ref, nw_ref, og_ref, sout_ref, s_ref):
    rows, c, sb = DELTA_ROWS, DN_CHUNK, DELTA_SUB
    n_sub, cps = rows // sb, sb // c
    shift = c.bit_length() - 1
    t = pl.program_id(1)

    @pl.when(t == 0)
    def _():
        s_ref[...] = s0_ref[0]

    bgv = bg_ref[...]
    ri = lax.broadcasted_iota(jnp.int32, (rows, rows), 0)
    ci = lax.broadcasted_iota(jnp.int32, (rows, rows), 1)
    same = (ri >> shift) == (ci >> shift)
    sel = jnp.concatenate([jnp.where(same & (ri >= ci), 1.0, 0.0), jnp.where(same, 1.0, 0.0)], axis=0).astype(BF16)
    g1, g2, g3 = _split3(bgv)
    cs = (jnp.dot(sel, g1, preferred_element_type=F32) + jnp.dot(sel, g2, preferred_element_type=F32)
          + jnp.dot(sel, g3, preferred_element_type=F32))
    gcs = cs[:rows]
    gls = cs[rows:]
    gcs_t = gcs.T

    rs = lax.broadcasted_iota(jnp.int32, (sb, sb), 0)
    cc = lax.broadcasted_iota(jnp.int32, (sb, sb), 1)
    same_s = (rs >> shift) == (cc >> shift)
    incl = same_s & (rs >= cc)
    strict = same_s & (rs > cc)

    probs = [(h, s) for s in range(n_sub) for h in range(DN_HEADS)]
    bcast = lambda col: jnp.broadcast_to(col, (sb, LANES))
    q, k, kbeta, decay, rhs, q_dec, kd_t, g_tot = {}, {}, {}, {}, {}, {}, {}, {}
    for p in probs:
        h, s = p
        rsl = slice(s * sb, (s + 1) * sb)
        hs = slice(h * LANES, (h + 1) * LANES)
        lane_g = DN_HEADS + h
        beta = bcast(bgv[rsl, h:h + 1])
        gc = bcast(gcs[rsl, lane_g:lane_g + 1])
        gl = bcast(gls[rsl, lane_g:lane_g + 1])
        egc = jnp.exp(gc)
        q[p] = q_ref[rsl, hs]
        k[p] = k_ref[rsl, hs]
        kbeta[p] = k[p] * beta
        decay[p] = jnp.where(incl, jnp.exp(jnp.where(incl, gc - gcs_t[lane_g:lane_g + 1, rsl], 0.0)), 0.0)
        rhs[p] = jnp.concatenate([v_ref[rsl, hs] * beta, kbeta[p] * egc], axis=1)
        q_dec[p] = q[p] * egc
        kd_t[p] = (k[p] * jnp.exp(gl - gc)).T
        g_tot[p] = jnp.exp(gl)
    a_mat = {p: jnp.where(strict, _mm_nt(kbeta[p], k[p]) * decay[p], 0.0) for p in probs}
    qk = {p: jnp.where(incl, _mm_nt(q[p], k[p]) * decay[p], 0.0) for p in probs}
    x_mat = {p: -a_mat[p] for p in probs}
    p_mat = a_mat
    for _ in range(shift - 1):
        p_mat = {p: _mm(p_mat[p], p_mat[p]) for p in probs}
        xp = {p: _mm(x_mat[p], p_mat[p]) for p in probs}
        x_mat = {p: x_mat[p] + p_mat[p] + xp[p] for p in probs}
    uw = {p: rhs[p] + _mm(x_mat[p], rhs[p]) for p in probs}

    col = lax.broadcasted_iota(jnp.int32, (DN_DK, sb), 1)
    zeros = jnp.zeros((c, DN_DV), F32)
    state = [s_ref[h] for h in range(DN_HEADS)]
    vn_parts = {p: [] for p in probs}
    qs_parts = {p: [] for p in probs}
    for j in range(rows // c):
        s, jj = divmod(j, cps)
        sl = slice(jj * c, (jj + 1) * c)
        r = {h: _mm(jnp.concatenate([uw[(h, s)][sl, DN_DV:], q_dec[(h, s)][sl]], axis=0), state[h])
             for h in range(DN_HEADS)}
        for h in range(DN_HEADS):
            p = (h, s)
            vn = uw[p][sl, :DN_DV] - r[h][:c]
            vn_parts[p].append(vn)
            qs_parts[p].append(r[h][c:])
            vn_pad = jnp.concatenate([vn if i == jj else zeros for i in range(cps)], axis=0)
            kd_j = jnp.where((col >= jj * c) & (col < (jj + 1) * c), kd_t[p], 0.0)
            state[h] = state[h] * g_tot[p][jj * c:jj * c + 1, :] + _mm(kd_j, vn_pad)
    for h in range(DN_HEADS):
        s_ref[h] = state[h]
    for p in probs:
        h, s = p
        o = jnp.concatenate(qs_parts[p], axis=0) + _mm(qk[p], jnp.concatenate(vn_parts[p], axis=0))
        o = o * lax.rsqrt(jnp.mean(o * o, -1, keepdims=True) + NORM_EPS) * nw_ref[...]
        rsl = slice(s * sb, (s + 1) * sb)
        hs = slice(h * LANES, (h + 1) * LANES)
        og_ref[rsl, hs] = (o * gz_ref[rsl, hs]).astype(BF16)

    @pl.when(t == pl.num_programs(1) - 1)
    def _():
        sout_ref[0] = s_ref[...]


def _delta_sample_kernel(q_ref, k_ref, v_ref, gz_ref, bg_ref, s0_ref, nw_ref, og_ref, sout_ref, *, tt):
    rows, c = DELTA_ROWS, tt
    nseq = rows // c
    head = pl.program_id(1)
    u, w, qk, q_dec, k_dec, g_tot = _delta_intra(q_ref[...], k_ref[...], v_ref[...], bg_ref[...], head, c)
    kd_t = k_dec.T
    col = lax.broadcasted_iota(jnp.int32, kd_t.shape, 1)
    row = lax.broadcasted_iota(jnp.int32, (rows, DN_DV), 0)
    vn_parts, qs_parts = [], []
    for i in range(nseq):
        sl = slice(i * c, (i + 1) * c)
        s = s0_ref[i, 0]
        r = _mm(jnp.concatenate([w[sl], q_dec[sl]], axis=0), s)
        vn_parts.append(u[sl] - r[:c])
        qs_parts.append(r[c:])
    vn_all = jnp.concatenate(vn_parts, axis=0)
    for i in range(nseq):
        kd_i = jnp.where((col >= i * c) & (col < (i + 1) * c), kd_t, 0.0)
        sout_ref[i, 0] = s0_ref[i, 0] * g_tot[i * c:i * c + 1, :] + _mm(kd_i, vn_all)
    o = jnp.concatenate(qs_parts, axis=0) + _mm(qk, vn_all)
    _delta_finish(o, gz_ref[...], nw_ref, og_ref)


def _delta(q, k, v, gz, bg, s0, norm_w, *, t_len):
    m = q.shape[0]
    bsz = m // t_len
    sds = jax.ShapeDtypeStruct
    out_shape = [sds((m, V_W), BF16), sds((bsz, DN_HEADS, DN_DK, DN_DV), F32)]
    if t_len % DELTA_ROWS == 0:
        n_t = t_len // DELTA_ROWS
        grid = (bsz, n_t)
        tok = lambda w: pl.BlockSpec((DELTA_ROWS, w), lambda b, t: (b * n_t + t, 0))
        st = pl.BlockSpec((1, DN_HEADS, DN_DK, DN_DV), lambda b, t: (b, 0, 0, 0))
        return pl.pallas_call(
            _delta_prompt_kernel, grid=grid,
            in_specs=[tok(QK_W), tok(QK_W), tok(V_W), tok(V_W), tok(LANES), st,
                      pl.BlockSpec((1, LANES), lambda b, t: (0, 0))],
            out_specs=[tok(V_W), st], out_shape=out_shape,
            scratch_shapes=[pltpu.VMEM((DN_HEADS, DN_DK, DN_DV), F32)],
            compiler_params=_params("parallel", "arbitrary"),
            name="delta_prompt",
        )(q, k, v, gz, bg, s0, norm_w)
    assert t_len <= DN_CHUNK and DELTA_ROWS % t_len == 0 and t_len % SUBLANES == 0
    nseq = DELTA_ROWS // t_len
    assert bsz % nseq == 0
    grid = (bsz // nseq, DN_HEADS)
    hd = pl.BlockSpec((DELTA_ROWS, LANES), lambda b, h: (b, h))
    st = pl.BlockSpec((nseq, 1, DN_DK, DN_DV), lambda b, h: (b, h, 0, 0))
    return pl.pallas_call(
        functools.partial(_delta_sample_kernel, tt=t_len), grid=grid,
        in_specs=[hd, hd, hd, hd, pl.BlockSpec((DELTA_ROWS, LANES), lambda b, h: (b, 0)), st,
                  pl.BlockSpec((1, LANES), lambda b, h: (0, 0))],
        out_specs=[hd, st], out_shape=out_shape,
        compiler_params=_params("parallel", "parallel"),
        name="delta_sample",
    )(q, k, v, gz, bg, s0, norm_w)


def _s5_perm(bb, tt):
    i = np.arange(bb * tt)
    gi, r = np.divmod(i, SUBLANES * tt)
    ti, bl = np.divmod(r, SUBLANES)
    perm = np.zeros((bb * tt, bb * tt), np.float32)
    perm[i, gi * SUBLANES * tt + bl * tt + ti] = 1.0
    return perm


def _s5_kernel(u_ref, h0re_ref, h0im_ref, perm_ref, permt_ref, wb_ref, lbre_ref, lbim_ref, wcre_ref, wcim_ref, d_ref,
               yb_ref, hre_ref, him_ref, hbuf_ref, hst_ref, *, bb, tt):
    rows = bb * tt
    t = pl.program_id(1)
    n_tiles = S5_LANES // LANES
    perm = perm_ref[...]
    u2 = sum(jnp.dot(perm, piece, preferred_element_type=F32) for piece in _split3(u_ref[...].reshape(rows, S5_CH)))
    u2b = u2.astype(BF16)
    half_t = n_tiles // 2
    for kt in range(2):
        bu = jnp.dot(u2b[:, kt * (S5_CH // 2):(kt + 1) * (S5_CH // 2)], wb_ref[kt], preferred_element_type=F32)
        for j in range(half_t):
            hbuf_ref[kt * half_t + j] = bu[:, j * LANES:(j + 1) * LANES]
            hbuf_ref[n_tiles + kt * half_t + j] = bu[:, (half_t + j) * LANES:(half_t + j + 1) * LANES]

    @pl.when(t == 0)
    def _():
        hst_ref[:, :, :S5_LANES] = h0re_ref[...].reshape(bb // SUBLANES, SUBLANES, S5_LANES)
        hst_ref[:, :, S5_LANES:] = h0im_ref[...].reshape(bb // SUBLANES, SUBLANES, S5_LANES)

    blk = 8
    for gi in range(bb // SUBLANES):
        for c0 in range(0, n_tiles, blk):
            lanes = [slice((c0 + j) * LANES, (c0 + j + 1) * LANES) for j in range(blk)]
            a_re = [lbre_ref[:, s] for s in lanes]
            a_im = [lbim_ref[:, s] for s in lanes]

            def step(i, carry):
                idx = pl.ds(pl.multiple_of((gi * tt + i) * SUBLANES, SUBLANES), SUBLANES)
                new = []
                for j in range(blk):
                    h_re, h_im = carry[j]
                    n_re = a_re[j] * h_re - a_im[j] * h_im + hbuf_ref[c0 + j, idx, :]
                    n_im = a_re[j] * h_im + a_im[j] * h_re + hbuf_ref[n_tiles + c0 + j, idx, :]
                    hbuf_ref[c0 + j, idx, :] = n_re
                    hbuf_ref[n_tiles + c0 + j, idx, :] = n_im
                    new.append((n_re, n_im))
                return tuple(new)

            init = tuple((hst_ref[gi, :, s], hst_ref[gi, :, slice(S5_LANES + s.start, S5_LANES + s.stop)])
                         for s in lanes)
            fin = lax.fori_loop(0, tt, step, init)
            for j, s in enumerate(lanes):
                hst_ref[gi, :, s] = fin[j][0]
                hst_ref[gi, :, slice(S5_LANES + s.start, S5_LANES + s.stop)] = fin[j][1]

    y_halves = []
    for n in range(2):
        h_re = jnp.concatenate([hbuf_ref[n * half_t + c] for c in range(half_t)], axis=1)
        h_im = jnp.concatenate([hbuf_ref[n_tiles + n * half_t + c] for c in range(half_t)], axis=1)
        y_halves.append(jnp.dot(h_re.astype(BF16), wcre_ref[n], preferred_element_type=F32)
                        - jnp.dot(h_im.astype(BF16), wcim_ref[n], preferred_element_type=F32))
    y = jnp.concatenate(y_halves, axis=1) + d_ref[...] * u2
    yb = jnp.dot(permt_ref[...], jax.nn.gelu(y).astype(BF16), preferred_element_type=F32)
    yb_ref[...] = yb.reshape(yb_ref.shape).astype(BF16)
    hre_ref[...] = hst_ref[:, :, :S5_LANES].reshape(bb, S5_LANES)
    him_ref[...] = hst_ref[:, :, S5_LANES:].reshape(bb, S5_LANES)


def _s5(u, h0_re, h0_im, w_b, lb_re, lb_im, wc_re, wc_im, d_row, *, t_len, bb, tt):
    m = u.shape[0]
    bsz = m // t_len
    grid = (bsz // bb, t_len // tt)
    if tt == t_len:
        view = (1, m, S5_CH)
        tok = pl.BlockSpec((1, bb * tt, S5_CH), lambda b, t: (0, b, 0))
    else:
        view = (bsz, t_len, S5_CH)
        tok = pl.BlockSpec((bb, tt, S5_CH), lambda b, t: (b, t, 0))
    full = lambda a: pl.BlockSpec(a.shape, lambda b, t: (0,) * a.ndim, pipeline_mode=pl.Buffered(1))
    st = pl.BlockSpec((bb, S5_LANES), lambda b, t: (b, 0))
    sds = jax.ShapeDtypeStruct
    perm = _s5_perm(bb, tt)
    perm, perm_t = jnp.asarray(perm, BF16), jnp.asarray(perm.T, BF16)
    yb, h_re, h_im = pl.pallas_call(
        functools.partial(_s5_kernel, bb=bb, tt=tt), grid=grid,
        in_specs=[tok, st, st, full(perm), full(perm_t), full(w_b), full(lb_re), full(lb_im), full(wc_re),
                  full(wc_im), full(d_row)],
        out_specs=[tok, st, st],
        out_shape=[sds(view, BF16), sds((bsz, S5_LANES), F32), sds((bsz, S5_LANES), F32)],
        scratch_shapes=[pltpu.VMEM((2 * S5_LANES // LANES, bb * tt, LANES), F32),
                        pltpu.VMEM((bb // SUBLANES, SUBLANES, 2 * S5_LANES), F32)],
        compiler_params=_params("parallel", "arbitrary"),
        name="s5",
    )(u.reshape(view), h0_re, h0_im, perm, perm_t, w_b, lb_re, lb_im, wc_re, wc_im, d_row)
    return yb.reshape(m, S5_CH), h_re, h_im


def _merge_kernel(x_ref, og_ref, yb_ref, wdn_ref, wglu_ref, wgab_ref, wout_ref, g_ref, b_ref, o_ref, *, alpha):
    x = x_ref[...]
    branch_a = jnp.dot(og_ref[...], wdn_ref[...], preferred_element_type=F32)
    glu = jnp.dot(yb_ref[...], wglu_ref[...], preferred_element_type=F32)
    branch_b = glu[:, :D_MODEL] * jax.nn.sigmoid(glu[:, D_MODEL:])
    gab = jnp.dot(x.astype(BF16), wgab_ref[...], preferred_element_type=F32)
    mix_in = jax.nn.sigmoid(gab[:, :D_MODEL]) * branch_a + jax.nn.sigmoid(gab[:, D_MODEL:]) * branch_b
    mix = jnp.dot(mix_in.astype(BF16), wout_ref[...], preferred_element_type=F32)
    o_ref[...] = _layer_norm(alpha * x + mix, g_ref[...], b_ref[...])


def _merge(x, og, yb, wdn, wglu, wgab, wout, ln_g, ln_b, *, alpha):
    m = x.shape[0]
    tm = MERGE_TILE
    row = lambda w: pl.BlockSpec((tm, w), lambda i: (i, 0))
    full = lambda a: pl.BlockSpec(a.shape, lambda i: (0,) * a.ndim, pipeline_mode=pl.Buffered(1))
    return pl.pallas_call(
        functools.partial(_merge_kernel, alpha=alpha), grid=(m // tm,),
        in_specs=[row(D_MODEL), row(V_W), row(S5_CH), full(wdn), full(wglu), full(wgab), full(wout),
                  full(ln_g), full(ln_b)],
        out_specs=row(D_MODEL), out_shape=jax.ShapeDtypeStruct((m, D_MODEL), F32),
        compiler_params=_params("parallel"),
        name="merge",
    )(x, og, yb, wdn, wglu, wgab, wout, ln_g, ln_b)


def _route(x, wr_hi_ref, wr_lo_ref, br_ref):
    x_hi, x_lo = _split2(x)
    logits = (jnp.dot(x_hi, wr_hi_ref[...], preferred_element_type=F32)
              + jnp.dot(x_hi, wr_lo_ref[...], preferred_element_type=F32)
              + jnp.dot(x_lo, wr_hi_ref[...], preferred_element_type=F32)) + br_ref[...]
    lane = lax.broadcasted_iota(jnp.int32, logits.shape, 1).astype(F32)
    first = lambda mask: jnp.min(jnp.where(mask, lane, float(LANES)), -1, keepdims=True)
    lg = jnp.where(lane < MOE_GROUPS, logits, NEG_BIG)
    g_max = jnp.max(lg, -1, keepdims=True)
    g_idx = first(lg == g_max)
    g_w = 1.0 / jnp.sum(jnp.where(lane < MOE_GROUPS, jnp.exp(lg - g_max), 0.0), -1, keepdims=True)
    lo = MOE_GROUPS + MOE_PER_GROUP * g_idx
    le = jnp.where((lane >= lo) & (lane < lo + MOE_PER_GROUP), logits, NEG_BIG)
    m1 = jnp.max(le, -1, keepdims=True)
    i1 = first(le == m1)
    le2 = jnp.where(lane == i1, NEG_BIG, le)
    m2 = jnp.max(le2, -1, keepdims=True)
    i2 = first(le2 == m2)
    e2 = jnp.exp(m2 - m1)
    w1 = g_w / (1.0 + e2)
    w2 = g_w * e2 / (1.0 + e2)
    return jnp.where(lane == i1, w1, 0.0) + jnp.where(lane == i2, w2, 0.0)


def _moe_kernel(x_ref, p_ref, wrh_ref, wrl_ref, br_ref, wgu_ref, wd_ref, g2_ref, b2_ref, wpg_ref, wple_ref,
                g3_ref, b3_ref, o_ref, acc_ref, comb_ref, xb_ref, *, alpha):
    e = pl.program_id(1)

    @pl.when(e == 0)
    def _():
        x = x_ref[...]
        comb_ref[...] = _route(x, wrh_ref, wrl_ref, br_ref)
        xb_ref[...] = x.astype(BF16)
        acc_ref[...] = jnp.zeros_like(acc_ref)

    xb = xb_ref[...]
    comb = comb_ref[...]
    lane = lax.broadcasted_iota(jnp.int32, comb.shape, 1)
    hs = []
    for i in range(MOE_PER_STEP):
        gu = jnp.dot(xb, wgu_ref[i], preferred_element_type=F32)
        c_e = jnp.sum(jnp.where(lane == e * MOE_PER_STEP + i + MOE_GROUPS, comb, 0.0), -1, keepdims=True)
        hs.append((_silu(gu[:, :MOE_HIDDEN]) * gu[:, MOE_HIDDEN:] * c_e).astype(BF16))
    acc_ref[...] += jnp.dot(jnp.concatenate(hs, axis=1), wd_ref[...], preferred_element_type=F32)

    @pl.when(e == MOE_EXPERTS // MOE_PER_STEP - 1)
    def _():
        x2 = _layer_norm(alpha * x_ref[...] + acc_ref[...], g2_ref[...], b2_ref[...])
        gate = jax.nn.sigmoid(jnp.dot(x2.astype(BF16), wpg_ref[...], preferred_element_type=F32))
        ple = gate * jnp.dot(p_ref[...].astype(BF16), wple_ref[...], preferred_element_type=F32)
        o_ref[...] = _layer_norm(alpha * x2 + ple, g3_ref[...], b3_ref[...])


def _moe(x, p, wr_hi, wr_lo, b_r, wgu, wd, g2, b2, wpg, wple, g3, b3, *, alpha):
    m = x.shape[0]
    tm = MOE_TILE
    row = lambda w: pl.BlockSpec((tm, w), lambda i, e: (i, 0))
    full = lambda a: pl.BlockSpec(a.shape, lambda i, e: (0,) * a.ndim, pipeline_mode=pl.Buffered(1))
    n_e = MOE_EXPERTS // MOE_PER_STEP
    expert_mode = dict(pipeline_mode=pl.Buffered(1)) if n_e == 1 else {}
    wd = wd.reshape(MOE_EXPERTS * MOE_HIDDEN, D_MODEL)
    return pl.pallas_call(
        functools.partial(_moe_kernel, alpha=alpha), grid=(m // tm, n_e),
        in_specs=[row(D_MODEL), row(PLE_DIM), full(wr_hi), full(wr_lo), full(b_r),
                  pl.BlockSpec((MOE_PER_STEP, D_MODEL, 2 * MOE_HIDDEN), lambda i, e: (e, 0, 0), **expert_mode),
                  pl.BlockSpec((MOE_PER_STEP * MOE_HIDDEN, D_MODEL), lambda i, e: (e, 0), **expert_mode),
                  full(g2), full(b2), full(wpg), full(wple), full(g3), full(b3)],
        out_specs=row(D_MODEL), out_shape=jax.ShapeDtypeStruct((m, D_MODEL), F32),
        scratch_shapes=[pltpu.VMEM((tm, D_MODEL), F32), pltpu.VMEM((tm, LANES), F32), pltpu.VMEM((tm, D_MODEL), BF16)],
        compiler_params=_params("parallel", "arbitrary"),
        name="moe",
    )(x, p, wr_hi, wr_lo, b_r, wgu, wd, g2, b2, wpg, wple, g3, b3)


def _prep_weights(i, w_in, conv_w, dn_a_log, dn_dt_bias, dn_norm_w, w_dn_out,
                  s5_lam_re, s5_lam_im, s5_log_dt, s5_b_re, s5_b_im, s5_c_re, s5_c_im, s5_d, w_glu,
                  w_out, ln1_g, ln1_b, w_rg, b_rg, w_re, b_re, w_gate, w_up, w_down, ln2_g, ln2_b,
                  w_ple, w_ple_gate, ln3_g, ln3_b):
    o_z = CONV_CH
    o_b = o_z + V_W
    o_u = o_b + 2 * DN_HEADS
    o_ga = o_u + S5_CH
    wi = w_in[i]
    row = lambda a, width=None: a.reshape(1, -1)
    head_pad = lambda a, off: jnp.zeros((1, LANES), F32).at[0, off:off + DN_HEADS].set(a)
    lb_re, lb_im, w_b = _s5_params(s5_lam_re[i], s5_lam_im[i], s5_log_dt[i], s5_b_re[i], s5_b_im[i])
    wc_re, wc_im = _s5_c_weights(s5_c_re[i], s5_c_im[i])
    w_router = jnp.zeros((D_MODEL, LANES), F32)
    w_router = w_router.at[:, :MOE_GROUPS].set(w_rg[i]).at[:, MOE_GROUPS:MOE_GROUPS + MOE_EXPERTS].set(w_re[i])
    b_router = jnp.zeros((1, LANES), F32)
    b_router = b_router.at[0, :MOE_GROUPS].set(b_rg[i]).at[0, MOE_GROUPS:MOE_GROUPS + MOE_EXPERTS].set(b_re[i])
    wr_hi = w_router.astype(BF16)
    wr_lo = (w_router - wr_hi.astype(F32)).astype(BF16)
    return dict(
        wqkv=wi[:, :CONV_CH].astype(BF16), wz=wi[:, o_z:o_b].astype(BF16),
        wba=jnp.pad(wi[:, o_b:o_u], ((0, 0), (0, LANES - 2 * DN_HEADS))).astype(BF16),
        wu=wi[:, o_u:o_ga].astype(BF16), wgab=wi[:, o_ga:].astype(BF16),
        conv_w=conv_w[i], alog=head_pad(dn_a_log[i], DN_HEADS), dtb=head_pad(dn_dt_bias[i], DN_HEADS),
        norm_w=row(dn_norm_w[i]), wdn=w_dn_out[i].astype(BF16),
        lb_re=lb_re, lb_im=lb_im, w_b=w_b, wc_re=wc_re, wc_im=wc_im, s5_d=row(s5_d[i]),
        wglu=w_glu[i].astype(BF16), wout=w_out[i].astype(BF16), ln1_g=row(ln1_g[i]), ln1_b=row(ln1_b[i]),
        wr_hi=wr_hi, wr_lo=wr_lo, b_r=b_router,
        wgu=jnp.concatenate([w_gate[i], w_up[i]], axis=-1).astype(BF16), wd=w_down[i].astype(BF16),
        ln2_g=row(ln2_g[i]), ln2_b=row(ln2_b[i]), wpg=w_ple_gate[i].astype(BF16), wple=w_ple[i].astype(BF16),
        ln3_g=row(ln3_g[i]), ln3_b=row(ln3_b[i]),
    )


def _layer(x, p, conv_buf, s_delta, h_re, h_im, w, alpha):
    bsz, t_len, _ = x.shape
    m = bsz * t_len
    assert t_len >= DN_CONV - 1
    if t_len % ROW_TILE == 0:
        bb, tt = 1, ROW_TILE
        s5_bb, s5_tt = SUBLANES, S5_ROWS // SUBLANES
    else:
        assert ROW_TILE % t_len == 0 and S5_ROWS % t_len == 0
        bb, tt = ROW_TILE // t_len, t_len
        s5_bb, s5_tt = S5_ROWS // t_len, t_len
    x2d = x.reshape(m, D_MODEL)
    q, k, v, gz, bg, u, conv_new = _inproj(x2d, conv_buf, w["wqkv"], w["wz"], w["wba"], w["wu"], w["conv_w"],
                                           w["alog"], w["dtb"], t_len=t_len, bb=bb, tt=tt)
    og, s_new = _delta(q, k, v, gz, bg, s_delta, w["norm_w"], t_len=t_len)
    yb, hr_new, hi_new = _s5(u, h_re.reshape(bsz, S5_LANES), h_im.reshape(bsz, S5_LANES), w["w_b"], w["lb_re"],
                             w["lb_im"], w["wc_re"], w["wc_im"], w["s5_d"], t_len=t_len, bb=s5_bb, tt=s5_tt)
    x1 = _merge(x2d, og, yb, w["wdn"], w["wglu"], w["wgab"], w["wout"], w["ln1_g"], w["ln1_b"], alpha=alpha)
    x3 = _moe(x1, p.reshape(m, PLE_DIM), w["wr_hi"], w["wr_lo"], w["b_r"], w["wgu"], w["wd"], w["ln2_g"],
              w["ln2_b"], w["wpg"], w["wple"], w["ln3_g"], w["ln3_b"], alpha=alpha)
    return (x3.reshape(bsz, t_len, D_MODEL), s_new, conv_new,
            hr_new.reshape(bsz, S5_GROUPS, S5_STATE), hi_new.reshape(bsz, S5_GROUPS, S5_STATE))


def kernel(x_prompt, x_sample, state_delta, state_conv, state_ssm_re, state_ssm_im, p_prompt, p_sample, w_in, conv_w, dn_a_log, dn_dt_bias, dn_norm_w, w_dn_out, s5_lam_re, s5_lam_im, s5_log_dt, s5_b_re, s5_b_im, s5_c_re, s5_c_im, s5_d, w_glu, w_out, ln1_g, ln1_b, w_rg, b_rg, w_re, b_re, w_gate, w_up, w_down, ln2_g, ln2_b, w_ple, w_ple_gate, ln3_g, ln3_b):
    weights = (w_in, conv_w, dn_a_log, dn_dt_bias, dn_norm_w, w_dn_out, s5_lam_re, s5_lam_im, s5_log_dt, s5_b_re,
               s5_b_im, s5_c_re, s5_c_im, s5_d, w_glu, w_out, ln1_g, ln1_b, w_rg, b_rg, w_re, b_re, w_gate, w_up,
               w_down, ln2_g, ln2_b, w_ple, w_ple_gate, ln3_g, ln3_b)
    depth = w_in.shape[0]
    alpha = (2 * depth) ** 0.25
    bp = x_prompt.shape[0]
    y_p, y_s = x_prompt, x_sample
    outs_p, outs_s = [], []
    for i in range(depth):
        w = _prep_weights(i, *weights)
        y_p, *st_p = _layer(y_p, p_prompt[i], jnp.zeros((bp, DN_CONV - 1, CONV_CH), F32),
                            jnp.zeros((bp, DN_HEADS, DN_DK, DN_DV), F32),
                            jnp.zeros((bp, S5_GROUPS, S5_STATE), F32), jnp.zeros((bp, S5_GROUPS, S5_STATE), F32),
                            w, alpha)
        y_s, *st_s = _layer(y_s, p_sample[i], state_conv[i], state_delta[i], state_ssm_re[i], state_ssm_im[i],
                            w, alpha)
        outs_p.append(st_p)
        outs_s.append(st_s)
    stack = lambda outs, j: jnp.stack([o[j] for o in outs])
    return (y_p, y_s, stack(outs_p, 0), stack(outs_p, 1), stack(outs_p, 2), stack(outs_p, 3),
            stack(outs_s, 0), stack(outs_s, 1), stack(outs_s, 2), stack(outs_s, 3))
```

```python
import functools

import jax
import jax.numpy as jnp
import numpy as np
from jax import lax
from jax.experimental import pallas as pl
from jax.experimental.pallas import tpu as pltpu

F32 = jnp.float32
BF16 = jnp.bfloat16

D_MODEL = 1024
DN_HEADS = 8
DN_DK = 128
DN_DV = 128
DN_CONV = 4
DN_CHUNK = 64
QK_W = DN_HEADS * DN_DK
V_W = DN_HEADS * DN_DV
CONV_CH = 2 * QK_W + V_W
S5_CH = 512
S5_GROUP = 16
S5_GROUPS = S5_CH // S5_GROUP
S5_STATE = 64
S5_LANES = S5_GROUPS * S5_STATE
MOE_GROUPS = 4
MOE_PER_GROUP = 4
MOE_EXPERTS = MOE_GROUPS * MOE_PER_GROUP
MOE_HIDDEN = 256
PLE_DIM = 256
LN_EPS = 1e-5
NORM_EPS = 1e-6

LANES = 128
SUBLANES = 8
VMEM_LIMIT_BYTES = 56 * 1024 * 1024

ROW_TILE = 256
MERGE_TILE = 256
MOE_TILE = 512
MOE_PER_STEP = 16
S5_ROWS = 512
DELTA_ROWS = 256
DELTA_SUB = 128
NEG_BIG = -1e30


def _params(*sem):
    return pltpu.CompilerParams(dimension_semantics=sem, vmem_limit_bytes=VMEM_LIMIT_BYTES)


def _mm(a, b):
    return jnp.dot(a.astype(BF16), b.astype(BF16), preferred_element_type=F32)


def _mm_nt(a, b):
    return lax.dot_general(a.astype(BF16), b.astype(BF16), (((1,), (1,)), ((), ())),
                           preferred_element_type=F32)


def _split3(x):
    x1 = x.astype(BF16)
    r1 = x - x1.astype(F32)
    x2 = r1.astype(BF16)
    r2 = r1 - x2.astype(F32)
    return x1, x2, r2.astype(BF16)


def _split2(x):
    x1 = x.astype(BF16)
    return x1, (x - x1.astype(F32)).astype(BF16)


def _silu(x):
    return x * jax.nn.sigmoid(x)


def _layer_norm(x, g, b):
    mu = jnp.mean(x, -1, keepdims=True)
    xc = x - mu
    var = jnp.mean(xc * xc, -1, keepdims=True)
    return xc * lax.rsqrt(var + LN_EPS) * g + b


def _s5_param_kernel(lre_ref, lim_ref, ldt_ref, bre_ref, bim_ref, lbre_ref, lbim_ref, bbre_ref, bbim_ref):
    lam_re = lre_ref[...]
    lam_im = lim_ref[...]
    dt = jnp.exp(ldt_ref[...])
    mag = jnp.exp(lam_re * dt)
    ang = lam_im * dt
    lb_re = mag * jnp.cos(ang)
    lb_im = mag * jnp.sin(ang)
    den = lam_re * lam_re + lam_im * lam_im
    nr = lb_re - 1.0
    ni = lb_im
    f_re = (nr * lam_re + ni * lam_im) / den
    f_im = (ni * lam_re - nr * lam_im) / den
    b_re = bre_ref[...]
    b_im = bim_ref[...]
    lbre_ref[...] = lb_re
    lbim_ref[...] = lb_im
    bbre_ref[...] = f_re * b_re - f_im * b_im
    bbim_ref[...] = f_re * b_im + f_im * b_re


def _s5_params(lam_re, lam_im, log_dt, b_re, b_im):
    row = lambda a: a.reshape(1, S5_LANES)
    ldt = jnp.repeat(log_dt, S5_STATE).reshape(1, S5_LANES)
    bt = lambda b: jnp.transpose(b, (2, 0, 1)).reshape(S5_GROUP, S5_LANES)
    sds = jax.ShapeDtypeStruct
    lb_re, lb_im, bb_re, bb_im = pl.pallas_call(
        _s5_param_kernel,
        out_shape=(sds((1, S5_LANES), F32), sds((1, S5_LANES), F32),
                   sds((S5_GROUP, S5_LANES), F32), sds((S5_GROUP, S5_LANES), F32)),
    )(row(lam_re), row(lam_im), ldt, bt(b_re), bt(b_im))
    eye = jnp.eye(S5_GROUPS, dtype=F32)

    def block_diag(bb):
        bb = bb.reshape(S5_GROUP, S5_GROUPS, S5_STATE)
        return (eye[:, None, :, None] * bb[None]).reshape(S5_CH, S5_LANES)

    hc, hl = S5_CH // 2, S5_LANES // 2
    quad = lambda w, kt: w[kt * hc:(kt + 1) * hc, kt * hl:(kt + 1) * hl]
    w_re, w_im = block_diag(bb_re), block_diag(bb_im)
    w_b = jnp.stack([jnp.concatenate([quad(w_re, kt), quad(w_im, kt)], axis=1) for kt in range(2)]).astype(BF16)
    return lb_re, lb_im, w_b


def _s5_c_weights(c_re, c_im):
    eye = jnp.eye(S5_GROUPS, dtype=F32)

    def block_diag(c):
        return (jnp.transpose(c, (0, 2, 1))[:, :, None, :] * eye[:, None, :, None]).reshape(S5_LANES, S5_CH)

    hc, hl = S5_CH // 2, S5_LANES // 2
    halves = lambda w: jnp.stack([w[n * hl:(n + 1) * hl, n * hc:(n + 1) * hc] for n in range(2)]).astype(BF16)
    return halves(block_diag(c_re)), halves(block_diag(c_im))


def _inproj_kernel(x_ref, cbuf_ref, wqkv_ref, wz_ref, wba_ref, wu_ref, convw_ref, alog_ref, dtb_ref,
                   q_ref, k_ref, v_ref, gz_ref, bg_ref, u_ref, cnew_ref, ext_ref, *, bb, tt):
    tm = bb * tt
    pad = SUBLANES
    t = pl.program_id(1)
    xb = x_ref[...].astype(BF16)

    @pl.when(t == 0)
    def _():
        ext_ref[:, pad - 3:pad, :] = cbuf_ref[...]

    pre = jnp.dot(xb, wqkv_ref[...], preferred_element_type=F32)
    ext_ref[:, pad:pad + tt, :] = pre.reshape(bb, tt, CONV_CH)

    for h in range(CONV_CH // LANES):
        cs = slice(h * LANES, (h + 1) * LANES)
        acc = ext_ref[:, pad:pad + tt, cs] * convw_ref[3:4, cs]
        for j in range(DN_CONV - 1):
            acc = acc + ext_ref[:, pad - 3 + j:pad - 3 + j + tt, cs] * convw_ref[j:j + 1, cs]
        y = _silu(acc).reshape(tm, LANES)
        if h < 2 * DN_HEADS:
            y = y * lax.rsqrt(jnp.sum(y * y, -1, keepdims=True) + NORM_EPS)
        if h < DN_HEADS:
            q_ref[:, cs] = y * (DN_DK ** -0.5)
        elif h < 2 * DN_HEADS:
            k_ref[:, slice(cs.start - QK_W, cs.stop - QK_W)] = y
        else:
            v_ref[:, slice(cs.start - 2 * QK_W, cs.stop - 2 * QK_W)] = y

    carry = ext_ref[:, pad + tt - 3:pad + tt, :]
    cnew_ref[...] = carry
    ext_ref[:, pad - 3:pad, :] = carry

    z = jnp.dot(xb, wz_ref[...], preferred_element_type=F32)
    gz_ref[...] = _silu(z)
    u_ref[...] = jnp.dot(xb, wu_ref[...], preferred_element_type=F32)

    ba = jnp.dot(xb, wba_ref[...], preferred_element_type=F32)
    lane = lax.broadcasted_iota(jnp.int32, ba.shape, 1)
    beta = jax.nn.sigmoid(ba)
    sp_in = ba + dtb_ref[...]
    softplus = jnp.maximum(sp_in, 0.0) + jnp.log1p(jnp.exp(-jnp.abs(sp_in)))
    g = -jnp.exp(alog_ref[...]) * softplus
    bg = jnp.where(lane < DN_HEADS, beta, jnp.where(lane < 2 * DN_HEADS, g, 0.0))
    bg_ref[...] = bg


def _inproj(x, conv_buf, wqkv, wz, wba, wu, conv_w, alog_row, dtb_row, *, t_len, bb, tt):
    m = x.shape[0]
    bsz = m // t_len
    n_t = t_len // tt
    grid = (bsz // bb, n_t)
    tok = lambda w: pl.BlockSpec((bb * tt, w), lambda b, t: (b * n_t + t, 0))
    full = lambda a: pl.BlockSpec(a.shape, lambda b, t: (0,) * a.ndim, pipeline_mode=pl.Buffered(1))
    sds = jax.ShapeDtypeStruct
    return pl.pallas_call(
        functools.partial(_inproj_kernel, bb=bb, tt=tt),
        grid=grid,
        in_specs=[tok(D_MODEL), pl.BlockSpec((bb, DN_CONV - 1, CONV_CH), lambda b, t: (b, 0, 0)),
                  full(wqkv), full(wz), full(wba), full(wu), full(conv_w), full(alog_row), full(dtb_row)],
        out_specs=[tok(QK_W), tok(QK_W), tok(V_W), tok(V_W), tok(LANES), tok(S5_CH),
                   pl.BlockSpec((bb, DN_CONV - 1, CONV_CH), lambda b, t: (b, 0, 0))],
        out_shape=[sds((m, QK_W), F32), sds((m, QK_W), F32), sds((m, V_W), F32), sds((m, V_W), F32),
                   sds((m, LANES), F32), sds((m, S5_CH), F32), sds((bsz, DN_CONV - 1, CONV_CH), F32)],
        scratch_shapes=[pltpu.VMEM((bb, SUBLANES + tt, CONV_CH), F32)],
        compiler_params=_params("parallel", "arbitrary"),
        name="inproj",
    )(x, conv_buf, wqkv, wz, wba, wu, conv_w, alog_row, dtb_row)


def _delta_local(q_ref, k_ref, v_ref, bg_ref, rows, c):
    sb = DELTA_SUB
    n_sub = rows // sb
    shift = c.bit_length() - 1

    bgv = bg_ref[...]
    ri = lax.broadcasted_iota(jnp.int32, (rows, rows), 0)
    ci = lax.broadcasted_iota(jnp.int32, (rows, rows), 1)
    same = (ri >> shift) == (ci >> shift)
    sel = jnp.concatenate([jnp.where(same & (ri >= ci), 1.0, 0.0), jnp.where(same, 1.0, 0.0)], axis=0).astype(BF16)
    g1, g2, g3 = _split3(bgv)
    cs = (jnp.dot(sel, g1, preferred_element_type=F32) + jnp.dot(sel, g2, preferred_element_type=F32)
          + jnp.dot(sel, g3, preferred_element_type=F32))
    gcs = cs[:rows]
    gls = cs[rows:]
    gcs_t = gcs.T

    rs = lax.broadcasted_iota(jnp.int32, (sb, sb), 0)
    cc = lax.broadcasted_iota(jnp.int32, (sb, sb), 1)
    same_s = (rs >> shift) == (cc >> shift)
    incl = same_s & (rs >= cc)
    strict = same_s & (rs > cc)

    probs = [(h, s) for s in range(n_sub) for h in range(DN_HEADS)]
    bcast = lambda col: jnp.broadcast_to(col, (sb, LANES))
    q, k, kbeta, decay, rhs, q_dec, kd_t, g_tot = {}, {}, {}, {}, {}, {}, {}, {}
    for p in probs:
        h, s = p
        rsl = slice(s * sb, (s + 1) * sb)
        hs = slice(h * LANES, (h + 1) * LANES)
        lane_g = DN_HEADS + h
        beta = bcast(bgv[rsl, h:h + 1])
        gc = bcast(gcs[rsl, lane_g:lane_g + 1])
        gl = bcast(gls[rsl, lane_g:lane_g + 1])
        egc = jnp.exp(gc)
        q[p] = q_ref[rsl, hs]
        k[p] = k_ref[rsl, hs]
        kbeta[p] = k[p] * beta
        decay[p] = jnp.where(incl, jnp.exp(jnp.where(incl, gc - gcs_t[lane_g:lane_g + 1, rsl], 0.0)), 0.0)
        rhs[p] = jnp.concatenate([v_ref[rsl, hs] * beta, kbeta[p] * egc], axis=1)
        q_dec[p] = q[p] * egc
        kd_t[p] = (k[p] * jnp.exp(gl - gc)).T
        g_tot[p] = jnp.exp(gl)
    a_mat = {p: jnp.where(strict, _mm_nt(kbeta[p], k[p]) * decay[p], 0.0) for p in probs}
    qk = {p: jnp.where(incl, _mm_nt(q[p], k[p]) * decay[p], 0.0) for p in probs}
    x_mat = {p: -a_mat[p] for p in probs}
    p_mat = a_mat
    for _ in range(shift - 1):
        p_mat = {p: _mm(p_mat[p], p_mat[p]) for p in probs}
        xp = {p: _mm(x_mat[p], p_mat[p]) for p in probs}
        x_mat = {p: x_mat[p] + p_mat[p] + xp[p] for p in probs}
    uw = {p: rhs[p] + _mm(x_mat[p], rhs[p]) for p in probs}
    return probs, uw, qk, q_dec, kd_t, g_tot


def _delta_store(o, p, gz_ref, nw_ref, og_ref):
    h, s = p
    o = o * lax.rsqrt(jnp.mean(o * o, -1, keepdims=True) + NORM_EPS) * nw_ref[...]
    rsl = slice(s * DELTA_SUB, (s + 1) * DELTA_SUB)
    hs = slice(h * LANES, (h + 1) * LANES)
    og_ref[rsl, hs] = (o * gz_ref[rsl, hs]).astype(BF16)


def _delta_prompt_kernel(q_ref, k_ref, v_ref, gz_ref, bg_ref, s0_ref, nw_ref, og_ref, sout_ref, s_ref):
    rows, c, sb = DELTA_ROWS, DN_CHUNK, DELTA_SUB
    cps = sb // c
    t = pl.program_id(1)

    @pl.when(t == 0)
    def _():
        s_ref[...] = s0_ref[0]

    probs, uw, qk, q_dec, kd_t, g_tot = _delta_local(q_ref, k_ref, v_ref, bg_ref, rows, c)
    col = lax.broadcasted_iota(jnp.int32, (DN_DK, sb), 1)
    zeros = jnp.zeros((c, DN_DV), F32)
    state = [s_ref[h] for h in range(DN_HEADS)]
    vn_parts = {p: [] for p in probs}
    qs_parts = {p: [] for p in probs}
    for j in range(rows // c):
        s, jj = divmod(j, cps)
        sl = slice(jj * c, (jj + 1) * c)
        r = {h: _mm(jnp.concatenate([uw[(h, s)][sl, DN_DV:], q_dec[(h, s)][sl]], axis=0), state[h])
             for h in range(DN_HEADS)}
        for h in range(DN_HEADS):
            p = (h, s)
            vn = uw[p][sl, :DN_DV] - r[h][:c]
            vn_parts[p].append(vn)
            qs_parts[p].append(r[h][c:])
            vn_pad = jnp.concatenate([vn if i == jj else zeros for i in range(cps)], axis=0)
            kd_j = jnp.where((col >= jj * c) & (col < (jj + 1) * c), kd_t[p], 0.0)
            state[h] = state[h] * g_tot[p][jj * c:jj * c + 1, :] + _mm(kd_j, vn_pad)
    for h in range(DN_HEADS):
        s_ref[h] = state[h]
    for p in probs:
        o = jnp.concatenate(qs_parts[p], axis=0) + _mm(qk[p], jnp.concatenate(vn_parts[p], axis=0))
        _delta_store(o, p, gz_ref, nw_ref, og_ref)

    @pl.when(t == pl.num_programs(1) - 1)
    def _():
        sout_ref[0] = s_ref[...]


def _delta_sample_kernel(q_ref, k_ref, v_ref, gz_ref, bg_ref, s0_ref, nw_ref, og_ref, sout_ref, *, tt):
    rows, c = DELTA_SUB, tt
    nseq = rows // c
    probs, uw, qk, q_dec, kd_t, g_tot = _delta_local(q_ref, k_ref, v_ref, bg_ref, rows, c)
    col = lax.broadcasted_iota(jnp.int32, (DN_DK, rows), 1)
    for p in probs:
        h = p[0]
        vn_parts, qs_parts = [], []
        for i in range(nseq):
            sl = slice(i * c, (i + 1) * c)
            r = _mm(jnp.concatenate([uw[p][sl, DN_DV:], q_dec[p][sl]], axis=0), s0_ref[i, h])
            vn_parts.append(uw[p][sl, :DN_DV] - r[:c])
            qs_parts.append(r[c:])
        vn_all = jnp.concatenate(vn_parts, axis=0)
        for i in range(nseq):
            kd_i = jnp.where((col >= i * c) & (col < (i + 1) * c), kd_t[p], 0.0)
            sout_ref[i, h] = s0_ref[i, h] * g_tot[p][i * c:i * c + 1, :] + _mm(kd_i, vn_all)
        _delta_store(jnp.concatenate(qs_parts, axis=0) + _mm(qk[p], vn_all), p, gz_ref, nw_ref, og_ref)


def _delta(q, k, v, gz, bg, s0, norm_w, *, t_len):
    m = q.shape[0]
    bsz = m // t_len
    sds = jax.ShapeDtypeStruct
    out_shape = [sds((m, V_W), BF16), sds((bsz, DN_HEADS, DN_DK, DN_DV), F32)]
    if t_len % DELTA_ROWS == 0:
        n_t = t_len // DELTA_ROWS
        grid = (bsz, n_t)
        tok = lambda w: pl.BlockSpec((DELTA_ROWS, w), lambda b, t: (b * n_t + t, 0))
        st = pl.BlockSpec((1, DN_HEADS, DN_DK, DN_DV), lambda b, t: (b, 0, 0, 0))
        return pl.pallas_call(
            _delta_prompt_kernel, grid=grid,
            in_specs=[tok(QK_W), tok(QK_W), tok(V_W), tok(V_W), tok(LANES), st,
                      pl.BlockSpec((1, LANES), lambda b, t: (0, 0))],
            out_specs=[tok(V_W), st], out_shape=out_shape,
            scratch_shapes=[pltpu.VMEM((DN_HEADS, DN_DK, DN_DV), F32)],
            compiler_params=_params("parallel", "arbitrary"),
            name="delta_prompt",
        )(q, k, v, gz, bg, s0, norm_w)
    assert t_len <= DN_CHUNK and DELTA_SUB % t_len == 0 and t_len % SUBLANES == 0
    nseq = DELTA_SUB // t_len
    assert bsz % nseq == 0
    tok = lambda w: pl.BlockSpec((DELTA_SUB, w), lambda b: (b, 0))
    st = pl.BlockSpec((nseq, DN_HEADS, DN_DK, DN_DV), lambda b: (b, 0, 0, 0))
    return pl.pallas_call(
        functools.partial(_delta_sample_kernel, tt=t_len), grid=(bsz // nseq,),
        in_specs=[tok(QK_W), tok(QK_W), tok(V_W), tok(V_W), tok(LANES), st,
                  pl.BlockSpec((1, LANES), lambda b: (0, 0))],
        out_specs=[tok(V_W), st], out_shape=out_shape,
        compiler_params=_params("parallel"),
        name="delta_sample",
    )(q, k, v, gz, bg, s0, norm_w)


def _s5_perm(bb, tt):
    i = np.arange(bb * tt)
    gi, r = np.divmod(i, SUBLANES * tt)
    ti, bl = np.divmod(r, SUBLANES)
    perm = np.zeros((bb * tt, bb * tt), np.float32)
    perm[i, gi * SUBLANES * tt + bl * tt + ti] = 1.0
    return perm


def _s5_kernel(u_ref, h0re_ref, h0im_ref, perm_ref, permt_ref, wb_ref, lbre_ref, lbim_ref, wcre_ref, wcim_ref, d_ref,
               yb_ref, hre_ref, him_ref, hbuf_ref, hst_ref, *, bb, tt):
    rows = bb * tt
    t = pl.program_id(1)
    n_tiles = S5_LANES // LANES
    perm = perm_ref[...]
    u2 = sum(jnp.dot(perm, piece, preferred_element_type=F32) for piece in _split3(u_ref[...].reshape(rows, S5_CH)))
    u2b = u2.astype(BF16)
    half_t = n_tiles // 2
    for kt in range(2):
        bu = jnp.dot(u2b[:, kt * (S5_CH // 2):(kt + 1) * (S5_CH // 2)], wb_ref[kt], preferred_element_type=F32)
        for j in range(half_t):
            hbuf_ref[kt * half_t + j] = bu[:, j * LANES:(j + 1) * LANES]
            hbuf_ref[n_tiles + kt * half_t + j] = bu[:, (half_t + j) * LANES:(half_t + j + 1) * LANES]

    @pl.when(t == 0)
    def _():
        hst_ref[:, :, :S5_LANES] = h0re_ref[...].reshape(bb // SUBLANES, SUBLANES, S5_LANES)
        hst_ref[:, :, S5_LANES:] = h0im_ref[...].reshape(bb // SUBLANES, SUBLANES, S5_LANES)

    blk = 8
    for gi in range(bb // SUBLANES):
        for c0 in range(0, n_tiles, blk):
            lanes = [slice((c0 + j) * LANES, (c0 + j + 1) * LANES) for j in range(blk)]
            a_re = [lbre_ref[:, s] for s in lanes]
            a_im = [lbim_ref[:, s] for s in lanes]

            def step(i, carry):
                idx = pl.ds(pl.multiple_of((gi * tt + i) * SUBLANES, SUBLANES), SUBLANES)
                new = []
                for j in range(blk):
                    h_re, h_im = carry[j]
                    n_re = a_re[j] * h_re - a_im[j] * h_im + hbuf_ref[c0 + j, idx, :]
                    n_im = a_re[j] * h_im + a_im[j] * h_re + hbuf_ref[n_tiles + c0 + j, idx, :]
                    hbuf_ref[c0 + j, idx, :] = n_re
                    hbuf_ref[n_tiles + c0 + j, idx, :] = n_im
                    new.append((n_re, n_im))
                return tuple(new)

            init = tuple((hst_ref[gi, :, s], hst_ref[gi, :, slice(S5_LANES + s.start, S5_LANES + s.stop)])
                         for s in lanes)
            fin = lax.fori_loop(0, tt, step, init)
            for j, s in enumerate(lanes):
                hst_ref[gi, :, s] = fin[j][0]
                hst_ref[gi, :, slice(S5_LANES + s.start, S5_LANES + s.stop)] = fin[j][1]

    y_halves = []
    for n in range(2):
        h_re = jnp.concatenate([hbuf_ref[n * half_t + c] for c in range(half_t)], axis=1)
        h_im = jnp.concatenate([hbuf_ref[n_tiles + n * half_t + c] for c in range(half_t)], axis=1)
        y_halves.append(jnp.dot(h_re.astype(BF16), wcre_ref[n], preferred_element_type=F32)
                        - jnp.dot(h_im.astype(BF16), wcim_ref[n], preferred_element_type=F32))
    y = jnp.concatenate(y_halves, axis=1) + d_ref[...] * u2
    yb = jnp.dot(permt_ref[...], jax.nn.gelu(y).astype(BF16), preferred_element_type=F32)
    yb_ref[...] = yb.reshape(yb_ref.shape).astype(BF16)
    hre_ref[...] = hst_ref[:, :, :S5_LANES].reshape(bb, S5_LANES)
    him_ref[...] = hst_ref[:, :, S5_LANES:].reshape(bb, S5_LANES)


def _s5(u, h0_re, h0_im, w_b, lb_re, lb_im, wc_re, wc_im, d_row, *, t_len, bb, tt):
    m = u.shape[0]
    bsz = m // t_len
    grid = (bsz // bb, t_len // tt)
    if tt == t_len:
        view = (1, m, S5_CH)
        tok = pl.BlockSpec((1, bb * tt, S5_CH), lambda b, t: (0, b, 0))
    else:
        view = (bsz, t_len, S5_CH)
        tok = pl.BlockSpec((bb, tt, S5_CH), lambda b, t: (b, t, 0))
    full = lambda a: pl.BlockSpec(a.shape, lambda b, t: (0,) * a.ndim, pipeline_mode=pl.Buffered(1))
    st = pl.BlockSpec((bb, S5_LANES), lambda b, t: (b, 0))
    sds = jax.ShapeDtypeStruct
    perm = _s5_perm(bb, tt)
    perm, perm_t = jnp.asarray(perm, BF16), jnp.asarray(perm.T, BF16)
    yb, h_re, h_im = pl.pallas_call(
        functools.partial(_s5_kernel, bb=bb, tt=tt), grid=grid,
        in_specs=[tok, st, st, full(perm), full(perm_t), full(w_b), full(lb_re), full(lb_im), full(wc_re),
                  full(wc_im), full(d_row)],
        out_specs=[tok, st, st],
        out_shape=[sds(view, BF16), sds((bsz, S5_LANES), F32), sds((bsz, S5_LANES), F32)],
        scratch_shapes=[pltpu.VMEM((2 * S5_LANES // LANES, bb * tt, LANES), F32),
                        pltpu.VMEM((bb // SUBLANES, SUBLANES, 2 * S5_LANES), F32)],
        compiler_params=_params("parallel", "arbitrary"),
        name="s5",
    )(u.reshape(view), h0_re, h0_im, perm, perm_t, w_b, lb_re, lb_im, wc_re, wc_im, d_row)
    return yb.reshape(m, S5_CH), h_re, h_im


def _merge_kernel(x_ref, og_ref, yb_ref, wdn_ref, wglu_ref, wgab_ref, wout_ref, g_ref, b_ref, o_ref, *, alpha):
    x = x_ref[...]
    branch_a = jnp.dot(og_ref[...], wdn_ref[...], preferred_element_type=F32)
    glu = jnp.dot(yb_ref[...], wglu_ref[...], preferred_element_type=F32)
    branch_b = glu[:, :D_MODEL] * jax.nn.sigmoid(glu[:, D_MODEL:])
    gab = jnp.dot(x.astype(BF16), wgab_ref[...], preferred_element_type=F32)
    mix_in = jax.nn.sigmoid(gab[:, :D_MODEL]) * branch_a + jax.nn.sigmoid(gab[:, D_MODEL:]) * branch_b
    mix = jnp.dot(mix_in.astype(BF16), wout_ref[...], preferred_element_type=F32)
    o_ref[...] = _layer_norm(alpha * x + mix, g_ref[...], b_ref[...])


def _merge(x, og, yb, wdn, wglu, wgab, wout, ln_g, ln_b, *, alpha):
    m = x.shape[0]
    tm = MERGE_TILE
    row = lambda w: pl.BlockSpec((tm, w), lambda i: (i, 0))
    full = lambda a: pl.BlockSpec(a.shape, lambda i: (0,) * a.ndim, pipeline_mode=pl.Buffered(1))
    return pl.pallas_call(
        functools.partial(_merge_kernel, alpha=alpha), grid=(m // tm,),
        in_specs=[row(D_MODEL), row(V_W), row(S5_CH), full(wdn), full(wglu), full(wgab), full(wout),
                  full(ln_g), full(ln_b)],
        out_specs=row(D_MODEL), out_shape=jax.ShapeDtypeStruct((m, D_MODEL), F32),
        compiler_params=_params("parallel"),
        name="merge",
    )(x, og, yb, wdn, wglu, wgab, wout, ln_g, ln_b)


def _route(x, wr_hi_ref, wr_lo_ref, br_ref):
    x_hi, x_lo = _split2(x)
    logits = (jnp.dot(x_hi, wr_hi_ref[...], preferred_element_type=F32)
              + jnp.dot(x_hi, wr_lo_ref[...], preferred_element_type=F32)
              + jnp.dot(x_lo, wr_hi_ref[...], preferred_element_type=F32)) + br_ref[...]
    lane = lax.broadcasted_iota(jnp.int32, logits.shape, 1).astype(F32)
    first = lambda mask: jnp.min(jnp.where(mask, lane, float(LANES)), -1, keepdims=True)
    lg = jnp.where(lane < MOE_GROUPS, logits, NEG_BIG)
    g_max = jnp.max(lg, -1, keepdims=True)
    g_idx = first(lg == g_max)
    g_w = 1.0 / jnp.sum(jnp.where(lane < MOE_GROUPS, jnp.exp(lg - g_max), 0.0), -1, keepdims=True)
    lo = MOE_GROUPS + MOE_PER_GROUP * g_idx
    le = jnp.where((lane >= lo) & (lane < lo + MOE_PER_GROUP), logits, NEG_BIG)
    m1 = jnp.max(le, -1, keepdims=True)
    i1 = first(le == m1)
    le2 = jnp.where(lane == i1, NEG_BIG, le)
    m2 = jnp.max(le2, -1, keepdims=True)
    i2 = first(le2 == m2)
    e2 = jnp.exp(m2 - m1)
    w1 = g_w / (1.0 + e2)
    w2 = g_w * e2 / (1.0 + e2)
    return jnp.where(lane == i1, w1, 0.0) + jnp.where(lane == i2, w2, 0.0)


def _moe_kernel(x_ref, p_ref, wrh_ref, wrl_ref, br_ref, wg_ref, wu_ref, wd_ref, g2_ref, b2_ref, wpg_ref, wple_ref,
                g3_ref, b3_ref, o_ref, acc_ref, comb_ref, xb_ref, *, alpha):
    e = pl.program_id(1)

    @pl.when(e == 0)
    def _():
        x = x_ref[...]
        comb_ref[...] = _route(x, wrh_ref, wrl_ref, br_ref)
        xb_ref[...] = x.astype(BF16)
        acc_ref[...] = jnp.zeros_like(acc_ref)

    xb = xb_ref[...]
    comb = comb_ref[...]
    lane = lax.broadcasted_iota(jnp.int32, comb.shape, 1)
    hs = []
    for i in range(MOE_PER_STEP):
        gate = jnp.dot(xb, wg_ref[i], preferred_element_type=F32)
        up = jnp.dot(xb, wu_ref[i], preferred_element_type=F32)
        c_e = jnp.sum(jnp.where(lane == e * MOE_PER_STEP + i + MOE_GROUPS, comb, 0.0), -1, keepdims=True)
        hs.append((_silu(gate) * up * c_e).astype(BF16))
    acc_ref[...] += jnp.dot(jnp.concatenate(hs, axis=1), wd_ref[...], preferred_element_type=F32)

    @pl.when(e == MOE_EXPERTS // MOE_PER_STEP - 1)
    def _():
        x2 = _layer_norm(alpha * x_ref[...] + acc_ref[...], g2_ref[...], b2_ref[...])
        gate = jax.nn.sigmoid(jnp.dot(x2.astype(BF16), wpg_ref[...], preferred_element_type=F32))
        ple = gate * jnp.dot(p_ref[...].astype(BF16), wple_ref[...], preferred_element_type=F32)
        o_ref[...] = _layer_norm(alpha * x2 + ple, g3_ref[...], b3_ref[...])


def _moe(x, p, wr_hi, wr_lo, b_r, wg, wu, wd, g2, b2, wpg, wple, g3, b3, *, alpha):
    m = x.shape[0]
    tm = MOE_TILE
    row = lambda w: pl.BlockSpec((tm, w), lambda i, e: (i, 0))
    full = lambda a: pl.BlockSpec(a.shape, lambda i, e: (0,) * a.ndim, pipeline_mode=pl.Buffered(1))
    n_e = MOE_EXPERTS // MOE_PER_STEP
    expert_mode = dict(pipeline_mode=pl.Buffered(1)) if n_e == 1 else {}
    wd = wd.reshape(MOE_EXPERTS * MOE_HIDDEN, D_MODEL)
    return pl.pallas_call(
        functools.partial(_moe_kernel, alpha=alpha), grid=(m // tm, n_e),
        in_specs=[row(D_MODEL), row(PLE_DIM), full(wr_hi), full(wr_lo), full(b_r),
                  pl.BlockSpec((MOE_PER_STEP, D_MODEL, MOE_HIDDEN), lambda i, e: (e, 0, 0), **expert_mode),
                  pl.BlockSpec((MOE_PER_STEP, D_MODEL, MOE_HIDDEN), lambda i, e: (e, 0, 0), **expert_mode),
                  pl.BlockSpec((MOE_PER_STEP * MOE_HIDDEN, D_MODEL), lambda i, e: (e, 0), **expert_mode),
                  full(g2), full(b2), full(wpg), full(wple), full(g3), full(b3)],
        out_specs=row(D_MODEL), out_shape=jax.ShapeDtypeStruct((m, D_MODEL), F32),
        scratch_shapes=[pltpu.VMEM((tm, D_MODEL), F32), pltpu.VMEM((tm, LANES), F32), pltpu.VMEM((tm, D_MODEL), BF16)],
        compiler_params=_params("parallel", "arbitrary"),
        name="moe",
    )(x, p, wr_hi, wr_lo, b_r, wg, wu, wd, g2, b2, wpg, wple, g3, b3)


def _prep_weights(i, w_in, conv_w, dn_a_log, dn_dt_bias, dn_norm_w, w_dn_out,
                  s5_lam_re, s5_lam_im, s5_log_dt, s5_b_re, s5_b_im, s5_c_re, s5_c_im, s5_d, w_glu,
                  w_out, ln1_g, ln1_b, w_rg, b_rg, w_re, b_re, w_gate, w_up, w_down, ln2_g, ln2_b,
                  w_ple, w_ple_gate, ln3_g, ln3_b):
    o_z = CONV_CH
    o_b = o_z + V_W
    o_u = o_b + 2 * DN_HEADS
    o_ga = o_u + S5_CH
    wi = w_in[i]
    row = lambda a, width=None: a.reshape(1, -1)
    head_pad = lambda a, off: jnp.zeros((1, LANES), F32).at[0, off:off + DN_HEADS].set(a)
    lb_re, lb_im, w_b = _s5_params(s5_lam_re[i], s5_lam_im[i], s5_log_dt[i], s5_b_re[i], s5_b_im[i])
    wc_re, wc_im = _s5_c_weights(s5_c_re[i], s5_c_im[i])
    w_router = jnp.zeros((D_MODEL, LANES), F32)
    w_router = w_router.at[:, :MOE_GROUPS].set(w_rg[i]).at[:, MOE_GROUPS:MOE_GROUPS + MOE_EXPERTS].set(w_re[i])
    b_router = jnp.zeros((1, LANES), F32)
    b_router = b_router.at[0, :MOE_GROUPS].set(b_rg[i]).at[0, MOE_GROUPS:MOE_GROUPS + MOE_EXPERTS].set(b_re[i])
    wr_hi = w_router.astype(BF16)
    wr_lo = (w_router - wr_hi.astype(F32)).astype(BF16)
    return dict(
        wqkv=wi[:, :CONV_CH].astype(BF16), wz=wi[:, o_z:o_b].astype(BF16),
        wba=jnp.pad(wi[:, o_b:o_u], ((0, 0), (0, LANES - 2 * DN_HEADS))).astype(BF16),
        wu=wi[:, o_u:o_ga].astype(BF16), wgab=wi[:, o_ga:].astype(BF16),
        conv_w=conv_w[i], alog=head_pad(dn_a_log[i], DN_HEADS), dtb=head_pad(dn_dt_bias[i], DN_HEADS),
        norm_w=row(dn_norm_w[i]), wdn=w_dn_out[i].astype(BF16),
        lb_re=lb_re, lb_im=lb_im, w_b=w_b, wc_re=wc_re, wc_im=wc_im, s5_d=row(s5_d[i]),
        wglu=w_glu[i].astype(BF16), wout=w_out[i].astype(BF16), ln1_g=row(ln1_g[i]), ln1_b=row(ln1_b[i]),
        wr_hi=wr_hi, wr_lo=wr_lo, b_r=b_router,
        wg=w_gate[i].astype(BF16), wup=w_up[i].astype(BF16), wd=w_down[i].astype(BF16),
        ln2_g=row(ln2_g[i]), ln2_b=row(ln2_b[i]), wpg=w_ple_gate[i].astype(BF16), wple=w_ple[i].astype(BF16),
        ln3_g=row(ln3_g[i]), ln3_b=row(ln3_b[i]),
    )


def _layer(x, p, conv_buf, s_delta, h_re, h_im, w, alpha):
    bsz, t_len, _ = x.shape
    m = bsz * t_len
    assert t_len >= DN_CONV - 1
    if t_len % ROW_TILE == 0:
        bb, tt = 1, ROW_TILE
        s5_bb, s5_tt = SUBLANES, S5_ROWS // SUBLANES
    else:
        assert ROW_TILE % t_len == 0 and S5_ROWS % t_len == 0
        bb, tt = ROW_TILE // t_len, t_len
        s5_bb, s5_tt = S5_ROWS // t_len, t_len
    x2d = x.reshape(m, D_MODEL)
    q, k, v, gz, bg, u, conv_new = _inproj(x2d, conv_buf, w["wqkv"], w["wz"], w["wba"], w["wu"], w["conv_w"],
                                           w["alog"], w["dtb"], t_len=t_len, bb=bb, tt=tt)
    og, s_new = _delta(q, k, v, gz, bg, s_delta, w["norm_w"], t_len=t_len)
    yb, hr_new, hi_new = _s5(u, h_re.reshape(bsz, S5_LANES), h_im.reshape(bsz, S5_LANES), w["w_b"], w["lb_re"],
                             w["lb_im"], w["wc_re"], w["wc_im"], w["s5_d"], t_len=t_len, bb=s5_bb, tt=s5_tt)
    x1 = _merge(x2d, og, yb, w["wdn"], w["wglu"], w["wgab"], w["wout"], w["ln1_g"], w["ln1_b"], alpha=alpha)
    x3 = _moe(x1, p.reshape(m, PLE_DIM), w["wr_hi"], w["wr_lo"], w["b_r"], w["wg"], w["wup"], w["wd"], w["ln2_g"],
              w["ln2_b"], w["wpg"], w["wple"], w["ln3_g"], w["ln3_b"], alpha=alpha)
    return (x3.reshape(bsz, t_len, D_MODEL), s_new, conv_new,
            hr_new.reshape(bsz, S5_GROUPS, S5_STATE), hi_new.reshape(bsz, S5_GROUPS, S5_STATE))


def kernel(x_prompt, x_sample, state_delta, state_conv, state_ssm_re, state_ssm_im, p_prompt, p_sample, w_in, conv_w, dn_a_log, dn_dt_bias, dn_norm_w, w_dn_out, s5_lam_re, s5_lam_im, s5_log_dt, s5_b_re, s5_b_im, s5_c_re, s5_c_im, s5_d, w_glu, w_out, ln1_g, ln1_b, w_rg, b_rg, w_re, b_re, w_gate, w_up, w_down, ln2_g, ln2_b, w_ple, w_ple_gate, ln3_g, ln3_b):
    weights = (w_in, conv_w, dn_a_log, dn_dt_bias, dn_norm_w, w_dn_out, s5_lam_re, s5_lam_im, s5_log_dt, s5_b_re,
               s5_b_im, s5_c_re, s5_c_im, s5_d, w_glu, w_out, ln1_g, ln1_b, w_rg, b_rg, w_re, b_re, w_gate, w_up,
               w_down, ln2_g, ln2_b, w_ple, w_ple_gate, ln3_g, ln3_b)
    depth = w_in.shape[0]
    alpha = (2 * depth) ** 0.25
    bp = x_prompt.shape[0]
    y_p, y_s = x_prompt, x_sample
    outs_p, outs_s = [], []
    for i in range(depth):
        w = _prep_weights(i, *weights)
        y_p, *st_p = _layer(y_p, p_prompt[i], jnp.zeros((bp, DN_CONV - 1, CONV_CH), F32),
                            jnp.zeros((bp, DN_HEADS, DN_DK, DN_DV), F32),
                            jnp.zeros((bp, S5_GROUPS, S5_STATE), F32), jnp.zeros((bp, S5_GROUPS, S5_STATE), F32),
                            w, alpha)
        y_s, *st_s = _layer(y_s, p_sample[i], state_conv[i], state_delta[i], state_ssm_re[i], state_ssm_im[i],
                            w, alpha)
        outs_p.append(st_p)
        outs_s.append(st_s)
    stack = lambda outs, j: jnp.stack([o[j] for o in outs])
    return (y_p, y_s, stack(outs_p, 0), stack(outs_p, 1), stack(outs_p, 2), stack(outs_p, 3),
            stack(outs_s, 0), stack(outs_s, 1), stack(outs_s, 2), stack(outs_s, 3))
```

```python
import functools

import jax
import jax.numpy as jnp
import numpy as np
from jax import lax
from jax.experimental import pallas as pl
from jax.experimental.pallas import tpu as pltpu

F32 = jnp.float32
BF16 = jnp.bfloat16

D_MODEL = 1024
DN_HEADS = 8
DN_DK = 128
DN_DV = 128
DN_CONV = 4
DN_CHUNK = 64
QK_W = DN_HEADS * DN_DK
V_W = DN_HEADS * DN_DV
CONV_CH = 2 * QK_W + V_W
S5_CH = 512
S5_GROUP = 16
S5_GROUPS = S5_CH // S5_GROUP
S5_STATE = 64
S5_LANES = S5_GROUPS * S5_STATE
MOE_GROUPS = 4
MOE_PER_GROUP = 4
MOE_EXPERTS = MOE_GROUPS * MOE_PER_GROUP
MOE_HIDDEN = 256
PLE_DIM = 256
LN_EPS = 1e-5
NORM_EPS = 1e-6

LANES = 128
SUBLANES = 8
VMEM_LIMIT_BYTES = 56 * 1024 * 1024

ROW_TILE = 256
MERGE_TILE = 256
MOE_TILE = 512
MOE_BLOCK = 128
S5_ROWS = 512
DELTA_ROWS = 256
DELTA_SUB = 128
NEG_BIG = -1e30


def _params(*sem):
    return pltpu.CompilerParams(dimension_semantics=sem, vmem_limit_bytes=VMEM_LIMIT_BYTES)


def _mm(a, b):
    return jnp.dot(a.astype(BF16), b.astype(BF16), preferred_element_type=F32)


def _mm_nt(a, b):
    return lax.dot_general(a.astype(BF16), b.astype(BF16), (((1,), (1,)), ((), ())),
                           preferred_element_type=F32)


def _split3(x):
    x1 = x.astype(BF16)
    r1 = x - x1.astype(F32)
    x2 = r1.astype(BF16)
    r2 = r1 - x2.astype(F32)
    return x1, x2, r2.astype(BF16)


def _split2(x):
    x1 = x.astype(BF16)
    return x1, (x - x1.astype(F32)).astype(BF16)


def _silu(x):
    return x * jax.nn.sigmoid(x)


def _layer_norm(x, g, b):
    mu = jnp.mean(x, -1, keepdims=True)
    xc = x - mu
    var = jnp.mean(xc * xc, -1, keepdims=True)
    return xc * lax.rsqrt(var + LN_EPS) * g + b


def _s5_param_kernel(lre_ref, lim_ref, ldt_ref, bre_ref, bim_ref, lbre_ref, lbim_ref, bbre_ref, bbim_ref):
    lam_re = lre_ref[...]
    lam_im = lim_ref[...]
    dt = jnp.exp(ldt_ref[...])
    mag = jnp.exp(lam_re * dt)
    ang = lam_im * dt
    lb_re = mag * jnp.cos(ang)
    lb_im = mag * jnp.sin(ang)
    den = lam_re * lam_re + lam_im * lam_im
    nr = lb_re - 1.0
    ni = lb_im
    f_re = (nr * lam_re + ni * lam_im) / den
    f_im = (ni * lam_re - nr * lam_im) / den
    b_re = bre_ref[...]
    b_im = bim_ref[...]
    lbre_ref[...] = lb_re
    lbim_ref[...] = lb_im
    bbre_ref[...] = f_re * b_re - f_im * b_im
    bbim_ref[...] = f_re * b_im + f_im * b_re


def _s5_params(lam_re, lam_im, log_dt, b_re, b_im):
    row = lambda a: a.reshape(1, S5_LANES)
    ldt = jnp.repeat(log_dt, S5_STATE).reshape(1, S5_LANES)
    bt = lambda b: jnp.transpose(b, (2, 0, 1)).reshape(S5_GROUP, S5_LANES)
    sds = jax.ShapeDtypeStruct
    lb_re, lb_im, bb_re, bb_im = pl.pallas_call(
        _s5_param_kernel,
        out_shape=(sds((1, S5_LANES), F32), sds((1, S5_LANES), F32),
                   sds((S5_GROUP, S5_LANES), F32), sds((S5_GROUP, S5_LANES), F32)),
    )(row(lam_re), row(lam_im), ldt, bt(b_re), bt(b_im))
    eye = jnp.eye(S5_GROUPS, dtype=F32)

    def block_diag(bb):
        bb = bb.reshape(S5_GROUP, S5_GROUPS, S5_STATE)
        return (eye[:, None, :, None] * bb[None]).reshape(S5_CH, S5_LANES)

    hc, hl = S5_CH // 2, S5_LANES // 2
    quad = lambda w, kt: w[kt * hc:(kt + 1) * hc, kt * hl:(kt + 1) * hl]
    w_re, w_im = block_diag(bb_re), block_diag(bb_im)
    w_b = jnp.stack([jnp.concatenate([quad(w_re, kt), quad(w_im, kt)], axis=1) for kt in range(2)]).astype(BF16)
    return lb_re, lb_im, w_b


def _s5_c_weights(c_re, c_im):
    eye = jnp.eye(S5_GROUPS, dtype=F32)

    def block_diag(c):
        return (jnp.transpose(c, (0, 2, 1))[:, :, None, :] * eye[:, None, :, None]).reshape(S5_LANES, S5_CH)

    hc, hl = S5_CH // 2, S5_LANES // 2
    halves = lambda w: jnp.stack([w[n * hl:(n + 1) * hl, n * hc:(n + 1) * hc] for n in range(2)]).astype(BF16)
    return halves(block_diag(c_re)), halves(block_diag(c_im))


def _inproj_kernel(x_ref, cbuf_ref, wqkv_ref, wz_ref, wba_ref, wu_ref, convw_ref, alog_ref, dtb_ref,
                   q_ref, k_ref, v_ref, gz_ref, bg_ref, u_ref, cnew_ref, ext_ref, *, bb, tt):
    tm = bb * tt
    pad = SUBLANES
    t = pl.program_id(1)
    xb = x_ref[...].astype(BF16)

    @pl.when(t == 0)
    def _():
        ext_ref[:, pad - 3:pad, :] = cbuf_ref[...]

    pre = jnp.dot(xb, wqkv_ref[...], preferred_element_type=F32)
    ext_ref[:, pad:pad + tt, :] = pre.reshape(bb, tt, CONV_CH)

    for h in range(CONV_CH // LANES):
        cs = slice(h * LANES, (h + 1) * LANES)
        acc = ext_ref[:, pad:pad + tt, cs] * convw_ref[3:4, cs]
        for j in range(DN_CONV - 1):
            acc = acc + ext_ref[:, pad - 3 + j:pad - 3 + j + tt, cs] * convw_ref[j:j + 1, cs]
        y = _silu(acc).reshape(tm, LANES)
        if h < 2 * DN_HEADS:
            y = y * lax.rsqrt(jnp.sum(y * y, -1, keepdims=True) + NORM_EPS)
        if h < DN_HEADS:
            q_ref[:, cs] = y * (DN_DK ** -0.5)
        elif h < 2 * DN_HEADS:
            k_ref[:, slice(cs.start - QK_W, cs.stop - QK_W)] = y
        else:
            v_ref[:, slice(cs.start - 2 * QK_W, cs.stop - 2 * QK_W)] = y

    carry = ext_ref[:, pad + tt - 3:pad + tt, :]
    cnew_ref[...] = carry
    ext_ref[:, pad - 3:pad, :] = carry

    z = jnp.dot(xb, wz_ref[...], preferred_element_type=F32)
    gz_ref[...] = _silu(z)
    u_ref[...] = jnp.dot(xb, wu_ref[...], preferred_element_type=F32)

    ba = jnp.dot(xb, wba_ref[...], preferred_element_type=F32)
    lane = lax.broadcasted_iota(jnp.int32, ba.shape, 1)
    beta = jax.nn.sigmoid(ba)
    sp_in = ba + dtb_ref[...]
    softplus = jnp.maximum(sp_in, 0.0) + jnp.log1p(jnp.exp(-jnp.abs(sp_in)))
    g = -jnp.exp(alog_ref[...]) * softplus
    bg = jnp.where(lane < DN_HEADS, beta, jnp.where(lane < 2 * DN_HEADS, g, 0.0))
    bg_ref[...] = bg


def _inproj(x, conv_buf, wqkv, wz, wba, wu, conv_w, alog_row, dtb_row, *, t_len, bb, tt):
    m = x.shape[0]
    bsz = m // t_len
    n_t = t_len // tt
    grid = (bsz // bb, n_t)
    tok = lambda w: pl.BlockSpec((bb * tt, w), lambda b, t: (b * n_t + t, 0))
    full = lambda a: pl.BlockSpec(a.shape, lambda b, t: (0,) * a.ndim, pipeline_mode=pl.Buffered(1))
    sds = jax.ShapeDtypeStruct
    return pl.pallas_call(
        functools.partial(_inproj_kernel, bb=bb, tt=tt),
        grid=grid,
        in_specs=[tok(D_MODEL), pl.BlockSpec((bb, DN_CONV - 1, CONV_CH), lambda b, t: (b, 0, 0)),
                  full(wqkv), full(wz), full(wba), full(wu), full(conv_w), full(alog_row), full(dtb_row)],
        out_specs=[tok(QK_W), tok(QK_W), tok(V_W), tok(V_W), tok(LANES), tok(S5_CH),
                   pl.BlockSpec((bb, DN_CONV - 1, CONV_CH), lambda b, t: (b, 0, 0))],
        out_shape=[sds((m, QK_W), F32), sds((m, QK_W), F32), sds((m, V_W), F32), sds((m, V_W), F32),
                   sds((m, LANES), F32), sds((m, S5_CH), F32), sds((bsz, DN_CONV - 1, CONV_CH), F32)],
        scratch_shapes=[pltpu.VMEM((bb, SUBLANES + tt, CONV_CH), F32)],
        compiler_params=_params("parallel", "arbitrary"),
        name="inproj",
    )(x, conv_buf, wqkv, wz, wba, wu, conv_w, alog_row, dtb_row)


def _delta_local(q_ref, k_ref, v_ref, bg_ref, rows, c):
    sb = DELTA_SUB
    n_sub = rows // sb
    shift = c.bit_length() - 1

    bgv = bg_ref[...]
    ri = lax.broadcasted_iota(jnp.int32, (rows, rows), 0)
    ci = lax.broadcasted_iota(jnp.int32, (rows, rows), 1)
    same = (ri >> shift) == (ci >> shift)
    sel = jnp.concatenate([jnp.where(same & (ri >= ci), 1.0, 0.0), jnp.where(same, 1.0, 0.0)], axis=0).astype(BF16)
    g1, g2, g3 = _split3(bgv)
    cs = (jnp.dot(sel, g1, preferred_element_type=F32) + jnp.dot(sel, g2, preferred_element_type=F32)
          + jnp.dot(sel, g3, preferred_element_type=F32))
    gcs = cs[:rows]
    gls = cs[rows:]
    gcs_t = gcs.T

    rs = lax.broadcasted_iota(jnp.int32, (sb, sb), 0)
    cc = lax.broadcasted_iota(jnp.int32, (sb, sb), 1)
    same_s = (rs >> shift) == (cc >> shift)
    incl = same_s & (rs >= cc)
    strict = same_s & (rs > cc)

    probs = [(h, s) for s in range(n_sub) for h in range(DN_HEADS)]
    bcast = lambda col: jnp.broadcast_to(col, (sb, LANES))
    q, k, kbeta, decay, rhs, q_dec, kd_t, g_tot = {}, {}, {}, {}, {}, {}, {}, {}
    for p in probs:
        h, s = p
        rsl = slice(s * sb, (s + 1) * sb)
        hs = slice(h * LANES, (h + 1) * LANES)
        lane_g = DN_HEADS + h
        beta = bcast(bgv[rsl, h:h + 1])
        gc = bcast(gcs[rsl, lane_g:lane_g + 1])
        gl = bcast(gls[rsl, lane_g:lane_g + 1])
        egc = jnp.exp(gc)
        q[p] = q_ref[rsl, hs]
        k[p] = k_ref[rsl, hs]
        kbeta[p] = k[p] * beta
        decay[p] = jnp.where(incl, jnp.exp(jnp.where(incl, gc - gcs_t[lane_g:lane_g + 1, rsl], 0.0)), 0.0)
        rhs[p] = jnp.concatenate([v_ref[rsl, hs] * beta, kbeta[p] * egc], axis=1)
        q_dec[p] = q[p] * egc
        kd_t[p] = (k[p] * jnp.exp(gl - gc)).T
        g_tot[p] = jnp.exp(gl)
    a_mat = {p: jnp.where(strict, _mm_nt(kbeta[p], k[p]) * decay[p], 0.0) for p in probs}
    qk = {p: jnp.where(incl, _mm_nt(q[p], k[p]) * decay[p], 0.0) for p in probs}
    x_mat = {p: -a_mat[p] for p in probs}
    p_mat = a_mat
    for _ in range(shift - 1):
        p_mat = {p: _mm(p_mat[p], p_mat[p]) for p in probs}
        xp = {p: _mm(x_mat[p], p_mat[p]) for p in probs}
        x_mat = {p: x_mat[p] + p_mat[p] + xp[p] for p in probs}
    uw = {p: rhs[p] + _mm(x_mat[p], rhs[p]) for p in probs}
    return probs, uw, qk, q_dec, kd_t, g_tot


def _delta_store(o, p, gz_ref, nw_ref, og_ref):
    h, s = p
    o = o * lax.rsqrt(jnp.mean(o * o, -1, keepdims=True) + NORM_EPS) * nw_ref[...]
    rsl = slice(s * DELTA_SUB, (s + 1) * DELTA_SUB)
    hs = slice(h * LANES, (h + 1) * LANES)
    og_ref[rsl, hs] = (o * gz_ref[rsl, hs]).astype(BF16)


def _delta_prompt_kernel(q_ref, k_ref, v_ref, gz_ref, bg_ref, s0_ref, nw_ref, og_ref, sout_ref, s_ref):
    rows, c, sb = DELTA_ROWS, DN_CHUNK, DELTA_SUB
    cps = sb // c
    t = pl.program_id(1)

    @pl.when(t == 0)
    def _():
        s_ref[...] = s0_ref[0]

    probs, uw, qk, q_dec, kd_t, g_tot = _delta_local(q_ref, k_ref, v_ref, bg_ref, rows, c)
    col = lax.broadcasted_iota(jnp.int32, (DN_DK, sb), 1)
    zeros = jnp.zeros((c, DN_DV), F32)
    state = [s_ref[h] for h in range(DN_HEADS)]
    vn_parts = {p: [] for p in probs}
    qs_parts = {p: [] for p in probs}
    for j in range(rows // c):
        s, jj = divmod(j, cps)
        sl = slice(jj * c, (jj + 1) * c)
        r = {h: _mm(jnp.concatenate([uw[(h, s)][sl, DN_DV:], q_dec[(h, s)][sl]], axis=0), state[h])
             for h in range(DN_HEADS)}
        for h in range(DN_HEADS):
            p = (h, s)
            vn = uw[p][sl, :DN_DV] - r[h][:c]
            vn_parts[p].append(vn)
            qs_parts[p].append(r[h][c:])
            vn_pad = jnp.concatenate([vn if i == jj else zeros for i in range(cps)], axis=0)
            kd_j = jnp.where((col >= jj * c) & (col < (jj + 1) * c), kd_t[p], 0.0)
            state[h] = state[h] * g_tot[p][jj * c:jj * c + 1, :] + _mm(kd_j, vn_pad)
    for h in range(DN_HEADS):
        s_ref[h] = state[h]
    for p in probs:
        o = jnp.concatenate(qs_parts[p], axis=0) + _mm(qk[p], jnp.concatenate(vn_parts[p], axis=0))
        _delta_store(o, p, gz_ref, nw_ref, og_ref)

    @pl.when(t == pl.num_programs(1) - 1)
    def _():
        sout_ref[0] = s_ref[...]


def _delta_sample_kernel(q_ref, k_ref, v_ref, gz_ref, bg_ref, s0_ref, nw_ref, og_ref, sout_ref, *, tt):
    rows, c = DELTA_SUB, tt
    nseq = rows // c
    probs, uw, qk, q_dec, kd_t, g_tot = _delta_local(q_ref, k_ref, v_ref, bg_ref, rows, c)
    col = lax.broadcasted_iota(jnp.int32, (DN_DK, rows), 1)
    for p in probs:
        h = p[0]
        vn_parts, qs_parts = [], []
        for i in range(nseq):
            sl = slice(i * c, (i + 1) * c)
            r = _mm(jnp.concatenate([uw[p][sl, DN_DV:], q_dec[p][sl]], axis=0), s0_ref[i, h])
            vn_parts.append(uw[p][sl, :DN_DV] - r[:c])
            qs_parts.append(r[c:])
        vn_all = jnp.concatenate(vn_parts, axis=0)
        for i in range(nseq):
            kd_i = jnp.where((col >= i * c) & (col < (i + 1) * c), kd_t[p], 0.0)
            sout_ref[i, h] = s0_ref[i, h] * g_tot[p][i * c:i * c + 1, :] + _mm(kd_i, vn_all)
        _delta_store(jnp.concatenate(qs_parts, axis=0) + _mm(qk[p], vn_all), p, gz_ref, nw_ref, og_ref)


def _delta(q, k, v, gz, bg, s0, norm_w, *, t_len):
    m = q.shape[0]
    bsz = m // t_len
    sds = jax.ShapeDtypeStruct
    out_shape = [sds((m, V_W), BF16), sds((bsz, DN_HEADS, DN_DK, DN_DV), F32)]
    if t_len % DELTA_ROWS == 0:
        n_t = t_len // DELTA_ROWS
        grid = (bsz, n_t)
        tok = lambda w: pl.BlockSpec((DELTA_ROWS, w), lambda b, t: (b * n_t + t, 0))
        st = pl.BlockSpec((1, DN_HEADS, DN_DK, DN_DV), lambda b, t: (b, 0, 0, 0))
        return pl.pallas_call(
            _delta_prompt_kernel, grid=grid,
            in_specs=[tok(QK_W), tok(QK_W), tok(V_W), tok(V_W), tok(LANES), st,
                      pl.BlockSpec((1, LANES), lambda b, t: (0, 0))],
            out_specs=[tok(V_W), st], out_shape=out_shape,
            scratch_shapes=[pltpu.VMEM((DN_HEADS, DN_DK, DN_DV), F32)],
            compiler_params=_params("parallel", "arbitrary"),
            name="delta_prompt",
        )(q, k, v, gz, bg, s0, norm_w)
    assert t_len <= DN_CHUNK and DELTA_SUB % t_len == 0 and t_len % SUBLANES == 0
    nseq = DELTA_SUB // t_len
    assert bsz % nseq == 0
    tok = lambda w: pl.BlockSpec((DELTA_SUB, w), lambda b: (b, 0))
    st = pl.BlockSpec((nseq, DN_HEADS, DN_DK, DN_DV), lambda b: (b, 0, 0, 0))
    return pl.pallas_call(
        functools.partial(_delta_sample_kernel, tt=t_len), grid=(bsz // nseq,),
        in_specs=[tok(QK_W), tok(QK_W), tok(V_W), tok(V_W), tok(LANES), st,
                  pl.BlockSpec((1, LANES), lambda b: (0, 0))],
        out_specs=[tok(V_W), st], out_shape=out_shape,
        compiler_params=_params("parallel"),
        name="delta_sample",
    )(q, k, v, gz, bg, s0, norm_w)


def _s5_perm(bb, tt):
    i = np.arange(bb * tt)
    gi, r = np.divmod(i, SUBLANES * tt)
    ti, bl = np.divmod(r, SUBLANES)
    perm = np.zeros((bb * tt, bb * tt), np.float32)
    perm[i, gi * SUBLANES * tt + bl * tt + ti] = 1.0
    return perm


def _s5_kernel(u_ref, h0re_ref, h0im_ref, perm_ref, permt_ref, wb_ref, lbre_ref, lbim_ref, wcre_ref, wcim_ref, d_ref,
               yb_ref, hre_ref, him_ref, hbuf_ref, hst_ref, *, bb, tt):
    rows = bb * tt
    t = pl.program_id(1)
    n_tiles = S5_LANES // LANES
    perm = perm_ref[...]
    u2 = sum(jnp.dot(perm, piece, preferred_element_type=F32) for piece in _split3(u_ref[...].reshape(rows, S5_CH)))
    u2b = u2.astype(BF16)
    half_t = n_tiles // 2
    for kt in range(2):
        bu = jnp.dot(u2b[:, kt * (S5_CH // 2):(kt + 1) * (S5_CH // 2)], wb_ref[kt], preferred_element_type=F32)
        for j in range(half_t):
            hbuf_ref[kt * half_t + j] = bu[:, j * LANES:(j + 1) * LANES]
            hbuf_ref[n_tiles + kt * half_t + j] = bu[:, (half_t + j) * LANES:(half_t + j + 1) * LANES]

    @pl.when(t == 0)
    def _():
        hst_ref[:, :, :S5_LANES] = h0re_ref[...].reshape(bb // SUBLANES, SUBLANES, S5_LANES)
        hst_ref[:, :, S5_LANES:] = h0im_ref[...].reshape(bb // SUBLANES, SUBLANES, S5_LANES)

    blk = 8
    for gi in range(bb // SUBLANES):
        for c0 in range(0, n_tiles, blk):
            lanes = [slice((c0 + j) * LANES, (c0 + j + 1) * LANES) for j in range(blk)]
            a_re = [lbre_ref[:, s] for s in lanes]
            a_im = [lbim_ref[:, s] for s in lanes]

            def step(i, carry):
                idx = pl.ds(pl.multiple_of((gi * tt + i) * SUBLANES, SUBLANES), SUBLANES)
                new = []
                for j in range(blk):
                    h_re, h_im = carry[j]
                    n_re = a_re[j] * h_re - a_im[j] * h_im + hbuf_ref[c0 + j, idx, :]
                    n_im = a_re[j] * h_im + a_im[j] * h_re + hbuf_ref[n_tiles + c0 + j, idx, :]
                    hbuf_ref[c0 + j, idx, :] = n_re
                    hbuf_ref[n_tiles + c0 + j, idx, :] = n_im
                    new.append((n_re, n_im))
                return tuple(new)

            init = tuple((hst_ref[gi, :, s], hst_ref[gi, :, slice(S5_LANES + s.start, S5_LANES + s.stop)])
                         for s in lanes)
            fin = lax.fori_loop(0, tt, step, init)
            for j, s in enumerate(lanes):
                hst_ref[gi, :, s] = fin[j][0]
                hst_ref[gi, :, slice(S5_LANES + s.start, S5_LANES + s.stop)] = fin[j][1]

    y_halves = []
    for n in range(2):
        h_re = jnp.concatenate([hbuf_ref[n * half_t + c] for c in range(half_t)], axis=1)
        h_im = jnp.concatenate([hbuf_ref[n_tiles + n * half_t + c] for c in range(half_t)], axis=1)
        y_halves.append(jnp.dot(h_re.astype(BF16), wcre_ref[n], preferred_element_type=F32)
                        - jnp.dot(h_im.astype(BF16), wcim_ref[n], preferred_element_type=F32))
    y = jnp.concatenate(y_halves, axis=1) + d_ref[...] * u2
    yb = jnp.dot(permt_ref[...], jax.nn.gelu(y).astype(BF16), preferred_element_type=F32)
    yb_ref[...] = yb.reshape(yb_ref.shape).astype(BF16)
    hre_ref[...] = hst_ref[:, :, :S5_LANES].reshape(bb, S5_LANES)
    him_ref[...] = hst_ref[:, :, S5_LANES:].reshape(bb, S5_LANES)


def _s5(u, h0_re, h0_im, w_b, lb_re, lb_im, wc_re, wc_im, d_row, *, t_len, bb, tt):
    m = u.shape[0]
    bsz = m // t_len
    grid = (bsz // bb, t_len // tt)
    if tt == t_len:
        view = (1, m, S5_CH)
        tok = pl.BlockSpec((1, bb * tt, S5_CH), lambda b, t: (0, b, 0))
    else:
        view = (bsz, t_len, S5_CH)
        tok = pl.BlockSpec((bb, tt, S5_CH), lambda b, t: (b, t, 0))
    full = lambda a: pl.BlockSpec(a.shape, lambda b, t: (0,) * a.ndim, pipeline_mode=pl.Buffered(1))
    st = pl.BlockSpec((bb, S5_LANES), lambda b, t: (b, 0))
    sds = jax.ShapeDtypeStruct
    perm = _s5_perm(bb, tt)
    perm, perm_t = jnp.asarray(perm, BF16), jnp.asarray(perm.T, BF16)
    yb, h_re, h_im = pl.pallas_call(
        functools.partial(_s5_kernel, bb=bb, tt=tt), grid=grid,
        in_specs=[tok, st, st, full(perm), full(perm_t), full(w_b), full(lb_re), full(lb_im), full(wc_re),
                  full(wc_im), full(d_row)],
        out_specs=[tok, st, st],
        out_shape=[sds(view, BF16), sds((bsz, S5_LANES), F32), sds((bsz, S5_LANES), F32)],
        scratch_shapes=[pltpu.VMEM((2 * S5_LANES // LANES, bb * tt, LANES), F32),
                        pltpu.VMEM((bb // SUBLANES, SUBLANES, 2 * S5_LANES), F32)],
        compiler_params=_params("parallel", "arbitrary"),
        name="s5",
    )(u.reshape(view), h0_re, h0_im, perm, perm_t, w_b, lb_re, lb_im, wc_re, wc_im, d_row)
    return yb.reshape(m, S5_CH), h_re, h_im


def _merge_kernel(x_ref, og_ref, yb_ref, wdn_ref, wglu_ref, wgab_ref, wout_ref, g_ref, b_ref, o_ref, *, alpha):
    x = x_ref[...]
    branch_a = jnp.dot(og_ref[...], wdn_ref[...], preferred_element_type=F32)
    glu = jnp.dot(yb_ref[...], wglu_ref[...], preferred_element_type=F32)
    branch_b = glu[:, :D_MODEL] * jax.nn.sigmoid(glu[:, D_MODEL:])
    gab = jnp.dot(x.astype(BF16), wgab_ref[...], preferred_element_type=F32)
    mix_in = jax.nn.sigmoid(gab[:, :D_MODEL]) * branch_a + jax.nn.sigmoid(gab[:, D_MODEL:]) * branch_b
    mix = jnp.dot(mix_in.astype(BF16), wout_ref[...], preferred_element_type=F32)
    o_ref[...] = _layer_norm(alpha * x + mix, g_ref[...], b_ref[...])


def _merge(x, og, yb, wdn, wglu, wgab, wout, ln_g, ln_b, *, alpha):
    m = x.shape[0]
    tm = MERGE_TILE
    row = lambda w: pl.BlockSpec((tm, w), lambda i: (i, 0))
    full = lambda a: pl.BlockSpec(a.shape, lambda i: (0,) * a.ndim, pipeline_mode=pl.Buffered(1))
    return pl.pallas_call(
        functools.partial(_merge_kernel, alpha=alpha), grid=(m // tm,),
        in_specs=[row(D_MODEL), row(V_W), row(S5_CH), full(wdn), full(wglu), full(wgab), full(wout),
                  full(ln_g), full(ln_b)],
        out_specs=row(D_MODEL), out_shape=jax.ShapeDtypeStruct((m, D_MODEL), F32),
        compiler_params=_params("parallel"),
        name="merge",
    )(x, og, yb, wdn, wglu, wgab, wout, ln_g, ln_b)


def _route(x, wr_hi_ref, wr_lo_ref, br_ref):
    x_hi, x_lo = _split2(x)
    logits = (jnp.dot(x_hi, wr_hi_ref[...], preferred_element_type=F32)
              + jnp.dot(x_hi, wr_lo_ref[...], preferred_element_type=F32)
              + jnp.dot(x_lo, wr_hi_ref[...], preferred_element_type=F32)) + br_ref[...]
    lane = lax.broadcasted_iota(jnp.int32, logits.shape, 1).astype(F32)
    first = lambda mask: jnp.min(jnp.where(mask, lane, float(LANES)), -1, keepdims=True)
    lg = jnp.where(lane < MOE_GROUPS, logits, NEG_BIG)
    g_max = jnp.max(lg, -1, keepdims=True)
    g_idx = first(lg == g_max)
    g_w = 1.0 / jnp.sum(jnp.where(lane < MOE_GROUPS, jnp.exp(lg - g_max), 0.0), -1, keepdims=True)
    lo = MOE_GROUPS + MOE_PER_GROUP * g_idx
    le = jnp.where((lane >= lo) & (lane < lo + MOE_PER_GROUP), logits, NEG_BIG)
    m1 = jnp.max(le, -1, keepdims=True)
    i1 = first(le == m1)
    le2 = jnp.where(lane == i1, NEG_BIG, le)
    m2 = jnp.max(le2, -1, keepdims=True)
    i2 = first(le2 == m2)
    e2 = jnp.exp(m2 - m1)
    w1 = g_w / (1.0 + e2)
    w2 = g_w * e2 / (1.0 + e2)
    return jnp.where(lane == i1, w1, 0.0) + jnp.where(lane == i2, w2, 0.0), g_idx


def _moe_kernel(x_ref, p_ref, wrh_ref, wrl_ref, br_ref, wg_ref, wu_ref, wd_ref, g2_ref, b2_ref, wpg_ref, wple_ref,
                g3_ref, b3_ref, o_ref, xs_ref, combs_ref, acc_ref, *, alpha):
    tm = x_ref.shape[0]
    x = x_ref[...]
    comb, g_idx = _route(x, wrh_ref, wrl_ref, br_ref)
    lane = lax.broadcasted_iota(jnp.int32, comb.shape, 1).astype(F32)
    onehot = jnp.where(lane == g_idx, 1.0, 0.0)
    ri = lax.broadcasted_iota(jnp.int32, (tm, tm), 0)
    ci = lax.broadcasted_iota(jnp.int32, (tm, tm), 1)
    before = jnp.where(ri > ci, 1.0, 0.0).astype(BF16)
    rank = jnp.dot(before, onehot.astype(BF16), preferred_element_type=F32)
    cnt = jnp.sum(onehot, axis=0, keepdims=True)
    lane_row = lane[:1]
    starts, start_row, run = [], jnp.zeros_like(cnt), jnp.zeros((1, 1), F32)
    for g in range(MOE_GROUPS):
        starts.append(run)
        start_row = start_row + jnp.where(lane_row == g, run, 0.0)
        run = run + cnt[:, g:g + 1]
    dest = jnp.sum(onehot * (rank + start_row), -1, keepdims=True)
    unsort = jnp.where(ci.astype(F32) == dest, 1.0, 0.0)
    sort = unsort.T.astype(BF16)
    unsort = unsort.astype(BF16)
    xs_ref[...] = jnp.dot(sort, x.astype(BF16), preferred_element_type=F32).astype(BF16)
    combs_ref[...] = sum(jnp.dot(sort, piece, preferred_element_type=F32) for piece in _split3(comb))
    acc_ref[...] = jnp.zeros_like(acc_ref)

    for b in range(tm // MOE_BLOCK):
        rows = slice(b * MOE_BLOCK, (b + 1) * MOE_BLOCK)
        for g in range(MOE_GROUPS):
            seg_lo = starts[g][0, 0]
            seg_hi = seg_lo + cnt[0, g]

            @pl.when((seg_lo < (b + 1) * MOE_BLOCK) & (seg_hi > b * MOE_BLOCK))
            def _():
                xb = xs_ref[rows, :]
                cb = combs_ref[rows, :]
                lane_b = lax.broadcasted_iota(jnp.int32, cb.shape, 1)
                hs = []
                for i in range(g * MOE_PER_GROUP, (g + 1) * MOE_PER_GROUP):
                    gate = jnp.dot(xb, wg_ref[i], preferred_element_type=F32)
                    up = jnp.dot(xb, wu_ref[i], preferred_element_type=F32)
                    c_e = jnp.sum(jnp.where(lane_b == i + MOE_GROUPS, cb, 0.0), -1, keepdims=True)
                    hs.append((_silu(gate) * up * c_e).astype(BF16))
                k0 = g * MOE_PER_GROUP * MOE_HIDDEN
                acc_ref[rows, :] += jnp.dot(jnp.concatenate(hs, axis=1), wd_ref[k0:k0 + MOE_PER_GROUP * MOE_HIDDEN, :],
                                            preferred_element_type=F32)

    moe = sum(jnp.dot(unsort, piece, preferred_element_type=F32) for piece in _split3(acc_ref[...]))
    x2 = _layer_norm(alpha * x + moe, g2_ref[...], b2_ref[...])
    gate = jax.nn.sigmoid(jnp.dot(x2.astype(BF16), wpg_ref[...], preferred_element_type=F32))
    ple = gate * jnp.dot(p_ref[...].astype(BF16), wple_ref[...], preferred_element_type=F32)
    o_ref[...] = _layer_norm(alpha * x2 + ple, g3_ref[...], b3_ref[...])


def _moe(x, p, wr_hi, wr_lo, b_r, wg, wu, wd, g2, b2, wpg, wple, g3, b3, *, alpha):
    m = x.shape[0]
    tm = MOE_TILE
    row = lambda w: pl.BlockSpec((tm, w), lambda i: (i, 0))
    full = lambda a: pl.BlockSpec(a.shape, lambda i: (0,) * a.ndim, pipeline_mode=pl.Buffered(1))
    wd = wd.reshape(MOE_EXPERTS * MOE_HIDDEN, D_MODEL)
    return pl.pallas_call(
        functools.partial(_moe_kernel, alpha=alpha), grid=(m // tm,),
        in_specs=[row(D_MODEL), row(PLE_DIM), full(wr_hi), full(wr_lo), full(b_r), full(wg), full(wu), full(wd),
                  full(g2), full(b2), full(wpg), full(wple), full(g3), full(b3)],
        out_specs=row(D_MODEL), out_shape=jax.ShapeDtypeStruct((m, D_MODEL), F32),
        scratch_shapes=[pltpu.VMEM((tm, D_MODEL), BF16), pltpu.VMEM((tm, LANES), F32), pltpu.VMEM((tm, D_MODEL), F32)],
        compiler_params=_params("parallel"),
        name="moe",
    )(x, p, wr_hi, wr_lo, b_r, wg, wu, wd, g2, b2, wpg, wple, g3, b3)


def _prep_weights(i, w_in, conv_w, dn_a_log, dn_dt_bias, dn_norm_w, w_dn_out,
                  s5_lam_re, s5_lam_im, s5_log_dt, s5_b_re, s5_b_im, s5_c_re, s5_c_im, s5_d, w_glu,
                  w_out, ln1_g, ln1_b, w_rg, b_rg, w_re, b_re, w_gate, w_up, w_down, ln2_g, ln2_b,
                  w_ple, w_ple_gate, ln3_g, ln3_b):
    o_z = CONV_CH
    o_b = o_z + V_W
    o_u = o_b + 2 * DN_HEADS
    o_ga = o_u + S5_CH
    wi = w_in[i]
    row = lambda a, width=None: a.reshape(1, -1)
    head_pad = lambda a, off: jnp.zeros((1, LANES), F32).at[0, off:off + DN_HEADS].set(a)
    lb_re, lb_im, w_b = _s5_params(s5_lam_re[i], s5_lam_im[i], s5_log_dt[i], s5_b_re[i], s5_b_im[i])
    wc_re, wc_im = _s5_c_weights(s5_c_re[i], s5_c_im[i])
    w_router = jnp.zeros((D_MODEL, LANES), F32)
    w_router = w_router.at[:, :MOE_GROUPS].set(w_rg[i]).at[:, MOE_GROUPS:MOE_GROUPS + MOE_EXPERTS].set(w_re[i])
    b_router = jnp.zeros((1, LANES), F32)
    b_router = b_router.at[0, :MOE_GROUPS].set(b_rg[i]).at[0, MOE_GROUPS:MOE_GROUPS + MOE_EXPERTS].set(b_re[i])
    wr_hi = w_router.astype(BF16)
    wr_lo = (w_router - wr_hi.astype(F32)).astype(BF16)
    return dict(
        wqkv=wi[:, :CONV_CH].astype(BF16), wz=wi[:, o_z:o_b].astype(BF16),
        wba=jnp.pad(wi[:, o_b:o_u], ((0, 0), (0, LANES - 2 * DN_HEADS))).astype(BF16),
        wu=wi[:, o_u:o_ga].astype(BF16), wgab=wi[:, o_ga:].astype(BF16),
        conv_w=conv_w[i], alog=head_pad(dn_a_log[i], DN_HEADS), dtb=head_pad(dn_dt_bias[i], DN_HEADS),
        norm_w=row(dn_norm_w[i]), wdn=w_dn_out[i].astype(BF16),
        lb_re=lb_re, lb_im=lb_im, w_b=w_b, wc_re=wc_re, wc_im=wc_im, s5_d=row(s5_d[i]),
        wglu=w_glu[i].astype(BF16), wout=w_out[i].astype(BF16), ln1_g=row(ln1_g[i]), ln1_b=row(ln1_b[i]),
        wr_hi=wr_hi, wr_lo=wr_lo, b_r=b_router,
        wg=w_gate[i].astype(BF16), wup=w_up[i].astype(BF16), wd=w_down[i].astype(BF16),
        ln2_g=row(ln2_g[i]), ln2_b=row(ln2_b[i]), wpg=w_ple_gate[i].astype(BF16), wple=w_ple[i].astype(BF16),
        ln3_g=row(ln3_g[i]), ln3_b=row(ln3_b[i]),
    )


def _layer(x, p, conv_buf, s_delta, h_re, h_im, w, alpha):
    bsz, t_len, _ = x.shape
    m = bsz * t_len
    assert t_len >= DN_CONV - 1
    if t_len % ROW_TILE == 0:
        bb, tt = 1, ROW_TILE
        s5_bb, s5_tt = SUBLANES, S5_ROWS // SUBLANES
    else:
        assert ROW_TILE % t_len == 0 and S5_ROWS % t_len == 0
        bb, tt = ROW_TILE // t_len, t_len
        s5_bb, s5_tt = S5_ROWS // t_len, t_len
    x2d = x.reshape(m, D_MODEL)
    q, k, v, gz, bg, u, conv_new = _inproj(x2d, conv_buf, w["wqkv"], w["wz"], w["wba"], w["wu"], w["conv_w"],
                                           w["alog"], w["dtb"], t_len=t_len, bb=bb, tt=tt)
    og, s_new = _delta(q, k, v, gz, bg, s_delta, w["norm_w"], t_len=t_len)
    yb, hr_new, hi_new = _s5(u, h_re.reshape(bsz, S5_LANES), h_im.reshape(bsz, S5_LANES), w["w_b"], w["lb_re"],
                             w["lb_im"], w["wc_re"], w["wc_im"], w["s5_d"], t_len=t_len, bb=s5_bb, tt=s5_tt)
    x1 = _merge(x2d, og, yb, w["wdn"], w["wglu"], w["wgab"], w["wout"], w["ln1_g"], w["ln1_b"], alpha=alpha)
    x3 = _moe(x1, p.reshape(m, PLE_DIM), w["wr_hi"], w["wr_lo"], w["b_r"], w["wg"], w["wup"], w["wd"], w["ln2_g"],
              w["ln2_b"], w["wpg"], w["wple"], w["ln3_g"], w["ln3_b"], alpha=alpha)
    return (x3.reshape(bsz, t_len, D_MODEL), s_new, conv_new,
            hr_new.reshape(bsz, S5_GROUPS, S5_STATE), hi_new.reshape(bsz, S5_GROUPS, S5_STATE))


def kernel(x_prompt, x_sample, state_delta, state_conv, state_ssm_re, state_ssm_im, p_prompt, p_sample, w_in, conv_w, dn_a_log, dn_dt_bias, dn_norm_w, w_dn_out, s5_lam_re, s5_lam_im, s5_log_dt, s5_b_re, s5_b_im, s5_c_re, s5_c_im, s5_d, w_glu, w_out, ln1_g, ln1_b, w_rg, b_rg, w_re, b_re, w_gate, w_up, w_down, ln2_g, ln2_b, w_ple, w_ple_gate, ln3_g, ln3_b):
    weights = (w_in, conv_w, dn_a_log, dn_dt_bias, dn_norm_w, w_dn_out, s5_lam_re, s5_lam_im, s5_log_dt, s5_b_re,
               s5_b_im, s5_c_re, s5_c_im, s5_d, w_glu, w_out, ln1_g, ln1_b, w_rg, b_rg, w_re, b_re, w_gate, w_up,
               w_down, ln2_g, ln2_b, w_ple, w_ple_gate, ln3_g, ln3_b)
    depth = w_in.shape[0]
    alpha = (2 * depth) ** 0.25
    bp = x_prompt.shape[0]
    y_p, y_s = x_prompt, x_sample
    outs_p, outs_s = [], []
    for i in range(depth):
        w = _prep_weights(i, *weights)
        y_p, *st_p = _layer(y_p, p_prompt[i], jnp.zeros((bp, DN_CONV - 1, CONV_CH), F32),
                            jnp.zeros((bp, DN_HEADS, DN_DK, DN_DV), F32),
                            jnp.zeros((bp, S5_GROUPS, S5_STATE), F32), jnp.zeros((bp, S5_GROUPS, S5_STATE), F32),
                            w, alpha)
        y_s, *st_s = _layer(y_s, p_sample[i], state_conv[i], state_delta[i], state_ssm_re[i], state_ssm_im[i],
                            w, alpha)
        outs_p.append(st_p)
        outs_s.append(st_s)
    stack = lambda outs, j: jnp.stack([o[j] for o in outs])
    return (y_p, y_s, stack(outs_p, 0), stack(outs_p, 1), stack(outs_p, 2), stack(outs_p, 3),
            stack(outs_s, 0), stack(outs_s, 1), stack(outs_s, 2), stack(outs_s, 3))
```

```python
import functools

import jax
import jax.numpy as jnp
import numpy as np
from jax import lax
from jax.experimental import pallas as pl
from jax.experimental.pallas import tpu as pltpu

F32 = jnp.float32
BF16 = jnp.bfloat16

D_MODEL = 1024
DN_HEADS = 8
DN_DK = 128
DN_DV = 128
DN_CONV = 4
DN_CHUNK = 64
QK_W = DN_HEADS * DN_DK
V_W = DN_HEADS * DN_DV
CONV_CH = 2 * QK_W + V_W
S5_CH = 512
S5_GROUP = 16
S5_GROUPS = S5_CH // S5_GROUP
S5_STATE = 64
S5_LANES = S5_GROUPS * S5_STATE
MOE_GROUPS = 4
MOE_PER_GROUP = 4
MOE_EXPERTS = MOE_GROUPS * MOE_PER_GROUP
MOE_HIDDEN = 256
PLE_DIM = 256
LN_EPS = 1e-5
NORM_EPS = 1e-6

LANES = 128
SUBLANES = 8
VMEM_LIMIT_BYTES = 56 * 1024 * 1024

ROW_TILE = 256
LONG_TILE = 512
MERGE_TILE = 256
MOE_TILE = 512
MOE_BLOCK = 128
S5_ROWS = 512
DELTA_ROWS = 256
DELTA_SUB = 128
NEG_BIG = -1e30


def _params(*sem):
    return pltpu.CompilerParams(dimension_semantics=sem, vmem_limit_bytes=VMEM_LIMIT_BYTES)


def _mm(a, b):
    return jnp.dot(a.astype(BF16), b.astype(BF16), preferred_element_type=F32)


def _mm_nt(a, b):
    return lax.dot_general(a.astype(BF16), b.astype(BF16), (((1,), (1,)), ((), ())),
                           preferred_element_type=F32)


def _split3(x):
    x1 = x.astype(BF16)
    r1 = x - x1.astype(F32)
    x2 = r1.astype(BF16)
    r2 = r1 - x2.astype(F32)
    return x1, x2, r2.astype(BF16)


def _split2(x):
    x1 = x.astype(BF16)
    return x1, (x - x1.astype(F32)).astype(BF16)


def _silu(x):
    return x * jax.nn.sigmoid(x)


def _layer_norm(x, g, b):
    mu = jnp.mean(x, -1, keepdims=True)
    xc = x - mu
    var = jnp.mean(xc * xc, -1, keepdims=True)
    return xc * lax.rsqrt(var + LN_EPS) * g + b


def _s5_param_kernel(lre_ref, lim_ref, ldt_ref, bre_ref, bim_ref, lbre_ref, lbim_ref, bbre_ref, bbim_ref):
    lam_re = lre_ref[...]
    lam_im = lim_ref[...]
    dt = jnp.exp(ldt_ref[...])
    mag = jnp.exp(lam_re * dt)
    ang = lam_im * dt
    lb_re = mag * jnp.cos(ang)
    lb_im = mag * jnp.sin(ang)
    den = lam_re * lam_re + lam_im * lam_im
    nr = lb_re - 1.0
    ni = lb_im
    f_re = (nr * lam_re + ni * lam_im) / den
    f_im = (ni * lam_re - nr * lam_im) / den
    b_re = bre_ref[...]
    b_im = bim_ref[...]
    lbre_ref[...] = lb_re
    lbim_ref[...] = lb_im
    bbre_ref[...] = f_re * b_re - f_im * b_im
    bbim_ref[...] = f_re * b_im + f_im * b_re


def _s5_params(lam_re, lam_im, log_dt, b_re, b_im):
    row = lambda a: a.reshape(1, S5_LANES)
    ldt = jnp.repeat(log_dt, S5_STATE).reshape(1, S5_LANES)
    bt = lambda b: jnp.transpose(b, (2, 0, 1)).reshape(S5_GROUP, S5_LANES)
    sds = jax.ShapeDtypeStruct
    lb_re, lb_im, bb_re, bb_im = pl.pallas_call(
        _s5_param_kernel,
        out_shape=(sds((1, S5_LANES), F32), sds((1, S5_LANES), F32),
                   sds((S5_GROUP, S5_LANES), F32), sds((S5_GROUP, S5_LANES), F32)),
    )(row(lam_re), row(lam_im), ldt, bt(b_re), bt(b_im))
    eye = jnp.eye(S5_GROUPS, dtype=F32)

    def block_diag(bb):
        bb = bb.reshape(S5_GROUP, S5_GROUPS, S5_STATE)
        return (eye[:, None, :, None] * bb[None]).reshape(S5_CH, S5_LANES)

    hc, hl = S5_CH // 2, S5_LANES // 2
    quad = lambda w, kt: w[kt * hc:(kt + 1) * hc, kt * hl:(kt + 1) * hl]
    w_re, w_im = block_diag(bb_re), block_diag(bb_im)
    w_b = jnp.stack([jnp.concatenate([quad(w_re, kt), quad(w_im, kt)], axis=1) for kt in range(2)]).astype(BF16)
    return lb_re, lb_im, w_b


def _s5_c_weights(c_re, c_im):
    eye = jnp.eye(S5_GROUPS, dtype=F32)

    def block_diag(c):
        return (jnp.transpose(c, (0, 2, 1))[:, :, None, :] * eye[:, None, :, None]).reshape(S5_LANES, S5_CH)

    hc, hl = S5_CH // 2, S5_LANES // 2
    halves = lambda w: jnp.stack([w[n * hl:(n + 1) * hl, n * hc:(n + 1) * hc] for n in range(2)]).astype(BF16)
    return halves(block_diag(c_re)), halves(block_diag(c_im))


def _inproj_kernel(x_ref, cbuf_ref, wqkv_ref, wz_ref, wba_ref, wu_ref, convw_ref, alog_ref, dtb_ref,
                   q_ref, k_ref, v_ref, gz_ref, bg_ref, u_ref, cnew_ref, ext_ref, *, bb, tt):
    tm = bb * tt
    pad = SUBLANES
    t = pl.program_id(1)
    xb = x_ref[...].astype(BF16)

    @pl.when(t == 0)
    def _():
        ext_ref[:, pad - 3:pad, :] = cbuf_ref[...]

    pre = jnp.dot(xb, wqkv_ref[...], preferred_element_type=F32)
    ext_ref[:, pad:pad + tt, :] = pre.reshape(bb, tt, CONV_CH)

    for h in range(CONV_CH // LANES):
        cs = slice(h * LANES, (h + 1) * LANES)
        acc = ext_ref[:, pad:pad + tt, cs] * convw_ref[3:4, cs]
        for j in range(DN_CONV - 1):
            acc = acc + ext_ref[:, pad - 3 + j:pad - 3 + j + tt, cs] * convw_ref[j:j + 1, cs]
        y = _silu(acc).reshape(tm, LANES)
        if h < 2 * DN_HEADS:
            y = y * lax.rsqrt(jnp.sum(y * y, -1, keepdims=True) + NORM_EPS)
        if h < DN_HEADS:
            q_ref[:, cs] = y * (DN_DK ** -0.5)
        elif h < 2 * DN_HEADS:
            k_ref[:, slice(cs.start - QK_W, cs.stop - QK_W)] = y
        else:
            v_ref[:, slice(cs.start - 2 * QK_W, cs.stop - 2 * QK_W)] = y

    carry = ext_ref[:, pad + tt - 3:pad + tt, :]
    cnew_ref[...] = carry
    ext_ref[:, pad - 3:pad, :] = carry

    z = jnp.dot(xb, wz_ref[...], preferred_element_type=F32)
    gz_ref[...] = _silu(z)
    u_ref[...] = jnp.dot(xb, wu_ref[...], preferred_element_type=F32)

    ba = jnp.dot(xb, wba_ref[...], preferred_element_type=F32)
    lane = lax.broadcasted_iota(jnp.int32, ba.shape, 1)
    beta = jax.nn.sigmoid(ba)
    sp_in = ba + dtb_ref[...]
    softplus = jnp.maximum(sp_in, 0.0) + jnp.log1p(jnp.exp(-jnp.abs(sp_in)))
    g = -jnp.exp(alog_ref[...]) * softplus
    bg = jnp.where(lane < DN_HEADS, beta, jnp.where(lane < 2 * DN_HEADS, g, 0.0))
    bg_ref[...] = bg


def _inproj(x, conv_buf, wqkv, wz, wba, wu, conv_w, alog_row, dtb_row, *, t_len, bb, tt):
    m = x.shape[0]
    bsz = m // t_len
    n_t = t_len // tt
    grid = (bsz // bb, n_t)
    tok = lambda w: pl.BlockSpec((bb * tt, w), lambda b, t: (b * n_t + t, 0))
    full = lambda a: pl.BlockSpec(a.shape, lambda b, t: (0,) * a.ndim, pipeline_mode=pl.Buffered(1))
    sds = jax.ShapeDtypeStruct
    return pl.pallas_call(
        functools.partial(_inproj_kernel, bb=bb, tt=tt),
        grid=grid,
        in_specs=[tok(D_MODEL), pl.BlockSpec((bb, DN_CONV - 1, CONV_CH), lambda b, t: (b, 0, 0)),
                  full(wqkv), full(wz), full(wba), full(wu), full(conv_w), full(alog_row), full(dtb_row)],
        out_specs=[tok(QK_W), tok(QK_W), tok(V_W), tok(V_W), tok(LANES), tok(S5_CH),
                   pl.BlockSpec((bb, DN_CONV - 1, CONV_CH), lambda b, t: (b, 0, 0))],
        out_shape=[sds((m, QK_W), F32), sds((m, QK_W), F32), sds((m, V_W), F32), sds((m, V_W), F32),
                   sds((m, LANES), F32), sds((m, S5_CH), F32), sds((bsz, DN_CONV - 1, CONV_CH), F32)],
        scratch_shapes=[pltpu.VMEM((bb, SUBLANES + tt, CONV_CH), F32)],
        compiler_params=_params("parallel", "arbitrary"),
        name="inproj",
    )(x, conv_buf, wqkv, wz, wba, wu, conv_w, alog_row, dtb_row)


def _delta_local(q_ref, k_ref, v_ref, bg_ref, rows, c):
    sb = DELTA_SUB
    n_sub = rows // sb
    shift = c.bit_length() - 1

    bgv = bg_ref[...]
    ri = lax.broadcasted_iota(jnp.int32, (rows, rows), 0)
    ci = lax.broadcasted_iota(jnp.int32, (rows, rows), 1)
    same = (ri >> shift) == (ci >> shift)
    sel = jnp.concatenate([jnp.where(same & (ri >= ci), 1.0, 0.0), jnp.where(same, 1.0, 0.0)], axis=0).astype(BF16)
    g1, g2, g3 = _split3(bgv)
    cs = (jnp.dot(sel, g1, preferred_element_type=F32) + jnp.dot(sel, g2, preferred_element_type=F32)
          + jnp.dot(sel, g3, preferred_element_type=F32))
    gcs = cs[:rows]
    gls = cs[rows:]
    gcs_t = gcs.T

    rs = lax.broadcasted_iota(jnp.int32, (sb, sb), 0)
    cc = lax.broadcasted_iota(jnp.int32, (sb, sb), 1)
    same_s = (rs >> shift) == (cc >> shift)
    incl = same_s & (rs >= cc)
    strict = same_s & (rs > cc)

    probs = [(h, s) for s in range(n_sub) for h in range(DN_HEADS)]
    bcast = lambda col: jnp.broadcast_to(col, (sb, LANES))
    q, k, kbeta, decay, rhs, q_dec, kd_t, g_tot = {}, {}, {}, {}, {}, {}, {}, {}
    for p in probs:
        h, s = p
        rsl = slice(s * sb, (s + 1) * sb)
        hs = slice(h * LANES, (h + 1) * LANES)
        lane_g = DN_HEADS + h
        beta = bcast(bgv[rsl, h:h + 1])
        gc = bcast(gcs[rsl, lane_g:lane_g + 1])
        gl = bcast(gls[rsl, lane_g:lane_g + 1])
        egc = jnp.exp(gc)
        q[p] = q_ref[rsl, hs]
        k[p] = k_ref[rsl, hs]
        kbeta[p] = k[p] * beta
        decay[p] = jnp.where(incl, jnp.exp(jnp.where(incl, gc - gcs_t[lane_g:lane_g + 1, rsl], 0.0)), 0.0)
        rhs[p] = jnp.concatenate([v_ref[rsl, hs] * beta, kbeta[p] * egc], axis=1)
        q_dec[p] = q[p] * egc
        kd_t[p] = (k[p] * jnp.exp(gl - gc)).T
        g_tot[p] = jnp.exp(gl)
    a_mat = {p: jnp.where(strict, _mm_nt(kbeta[p], k[p]) * decay[p], 0.0) for p in probs}
    qk = {p: jnp.where(incl, _mm_nt(q[p], k[p]) * decay[p], 0.0) for p in probs}
    x_mat = {p: -a_mat[p] for p in probs}
    p_mat = a_mat
    for _ in range(shift - 1):
        p_mat = {p: _mm(p_mat[p], p_mat[p]) for p in probs}
        xp = {p: _mm(x_mat[p], p_mat[p]) for p in probs}
        x_mat = {p: x_mat[p] + p_mat[p] + xp[p] for p in probs}
    uw = {p: rhs[p] + _mm(x_mat[p], rhs[p]) for p in probs}
    return probs, uw, qk, q_dec, kd_t, g_tot


def _delta_store(o, p, gz_ref, nw_ref, og_ref):
    h, s = p
    o = o * lax.rsqrt(jnp.mean(o * o, -1, keepdims=True) + NORM_EPS) * nw_ref[...]
    rsl = slice(s * DELTA_SUB, (s + 1) * DELTA_SUB)
    hs = slice(h * LANES, (h + 1) * LANES)
    og_ref[rsl, hs] = (o * gz_ref[rsl, hs]).astype(BF16)


def _delta_prompt_kernel(q_ref, k_ref, v_ref, gz_ref, bg_ref, s0_ref, nw_ref, og_ref, sout_ref, s_ref):
    rows, c, sb = DELTA_ROWS, DN_CHUNK, DELTA_SUB
    cps = sb // c
    t = pl.program_id(1)

    @pl.when(t == 0)
    def _():
        s_ref[...] = s0_ref[0]

    probs, uw, qk, q_dec, kd_t, g_tot = _delta_local(q_ref, k_ref, v_ref, bg_ref, rows, c)
    col = lax.broadcasted_iota(jnp.int32, (DN_DK, sb), 1)
    zeros = jnp.zeros((c, DN_DV), F32)
    state = [s_ref[h] for h in range(DN_HEADS)]
    vn_parts = {p: [] for p in probs}
    qs_parts = {p: [] for p in probs}
    for j in range(rows // c):
        s, jj = divmod(j, cps)
        sl = slice(jj * c, (jj + 1) * c)
        r = {h: _mm(jnp.concatenate([uw[(h, s)][sl, DN_DV:], q_dec[(h, s)][sl]], axis=0), state[h])
             for h in range(DN_HEADS)}
        for h in range(DN_HEADS):
            p = (h, s)
            vn = uw[p][sl, :DN_DV] - r[h][:c]
            vn_parts[p].append(vn)
            qs_parts[p].append(r[h][c:])
            vn_pad = jnp.concatenate([vn if i == jj else zeros for i in range(cps)], axis=0)
            kd_j = jnp.where((col >= jj * c) & (col < (jj + 1) * c), kd_t[p], 0.0)
            state[h] = state[h] * g_tot[p][jj * c:jj * c + 1, :] + _mm(kd_j, vn_pad)
    for h in range(DN_HEADS):
        s_ref[h] = state[h]
    for p in probs:
        o = jnp.concatenate(qs_parts[p], axis=0) + _mm(qk[p], jnp.concatenate(vn_parts[p], axis=0))
        _delta_store(o, p, gz_ref, nw_ref, og_ref)

    @pl.when(t == pl.num_programs(1) - 1)
    def _():
        sout_ref[0] = s_ref[...]


def _delta_sample_kernel(q_ref, k_ref, v_ref, gz_ref, bg_ref, s0_ref, nw_ref, og_ref, sout_ref, *, tt):
    rows, c = DELTA_SUB, tt
    nseq = rows // c
    probs, uw, qk, q_dec, kd_t, g_tot = _delta_local(q_ref, k_ref, v_ref, bg_ref, rows, c)
    col = lax.broadcasted_iota(jnp.int32, (DN_DK, rows), 1)
    for p in probs:
        h = p[0]
        vn_parts, qs_parts = [], []
        for i in range(nseq):
            sl = slice(i * c, (i + 1) * c)
            r = _mm(jnp.concatenate([uw[p][sl, DN_DV:], q_dec[p][sl]], axis=0), s0_ref[i, h])
            vn_parts.append(uw[p][sl, :DN_DV] - r[:c])
            qs_parts.append(r[c:])
        vn_all = jnp.concatenate(vn_parts, axis=0)
        for i in range(nseq):
            kd_i = jnp.where((col >= i * c) & (col < (i + 1) * c), kd_t[p], 0.0)
            sout_ref[i, h] = s0_ref[i, h] * g_tot[p][i * c:i * c + 1, :] + _mm(kd_i, vn_all)
        _delta_store(jnp.concatenate(qs_parts, axis=0) + _mm(qk[p], vn_all), p, gz_ref, nw_ref, og_ref)


def _delta(q, k, v, gz, bg, s0, norm_w, *, t_len):
    m = q.shape[0]
    bsz = m // t_len
    sds = jax.ShapeDtypeStruct
    out_shape = [sds((m, V_W), BF16), sds((bsz, DN_HEADS, DN_DK, DN_DV), F32)]
    if t_len % DELTA_ROWS == 0:
        n_t = t_len // DELTA_ROWS
        grid = (bsz, n_t)
        tok = lambda w: pl.BlockSpec((DELTA_ROWS, w), lambda b, t: (b * n_t + t, 0))
        st = pl.BlockSpec((1, DN_HEADS, DN_DK, DN_DV), lambda b, t: (b, 0, 0, 0))
        return pl.pallas_call(
            _delta_prompt_kernel, grid=grid,
            in_specs=[tok(QK_W), tok(QK_W), tok(V_W), tok(V_W), tok(LANES), st,
                      pl.BlockSpec((1, LANES), lambda b, t: (0, 0))],
            out_specs=[tok(V_W), st], out_shape=out_shape,
            scratch_shapes=[pltpu.VMEM((DN_HEADS, DN_DK, DN_DV), F32)],
            compiler_params=_params("parallel", "arbitrary"),
            name="delta_prompt",
        )(q, k, v, gz, bg, s0, norm_w)
    assert t_len <= DN_CHUNK and DELTA_SUB % t_len == 0 and t_len % SUBLANES == 0
    nseq = DELTA_SUB // t_len
    assert bsz % nseq == 0
    tok = lambda w: pl.BlockSpec((DELTA_SUB, w), lambda b: (b, 0))
    st = pl.BlockSpec((nseq, DN_HEADS, DN_DK, DN_DV), lambda b: (b, 0, 0, 0))
    return pl.pallas_call(
        functools.partial(_delta_sample_kernel, tt=t_len), grid=(bsz // nseq,),
        in_specs=[tok(QK_W), tok(QK_W), tok(V_W), tok(V_W), tok(LANES), st,
                  pl.BlockSpec((1, LANES), lambda b: (0, 0))],
        out_specs=[tok(V_W), st], out_shape=out_shape,
        compiler_params=_params("parallel"),
        name="delta_sample",
    )(q, k, v, gz, bg, s0, norm_w)


def _s5_perm(bb, tt):
    i = np.arange(bb * tt)
    gi, r = np.divmod(i, SUBLANES * tt)
    ti, bl = np.divmod(r, SUBLANES)
    perm = np.zeros((bb * tt, bb * tt), np.float32)
    perm[i, gi * SUBLANES * tt + bl * tt + ti] = 1.0
    return perm


def _s5_kernel(u_ref, h0re_ref, h0im_ref, perm_ref, permt_ref, wb_ref, lbre_ref, lbim_ref, wcre_ref, wcim_ref, d_ref,
               yb_ref, hre_ref, him_ref, hbuf_ref, hst_ref, *, bb, tt):
    rows = bb * tt
    t = pl.program_id(1)
    n_tiles = S5_LANES // LANES
    perm = perm_ref[...]
    u2 = sum(jnp.dot(perm, piece, preferred_element_type=F32) for piece in _split3(u_ref[...].reshape(rows, S5_CH)))
    u2b = u2.astype(BF16)
    half_t = n_tiles // 2
    for kt in range(2):
        bu = jnp.dot(u2b[:, kt * (S5_CH // 2):(kt + 1) * (S5_CH // 2)], wb_ref[kt], preferred_element_type=F32)
        for j in range(half_t):
            hbuf_ref[kt * half_t + j] = bu[:, j * LANES:(j + 1) * LANES]
            hbuf_ref[n_tiles + kt * half_t + j] = bu[:, (half_t + j) * LANES:(half_t + j + 1) * LANES]

    @pl.when(t == 0)
    def _():
        hst_ref[:, :, :S5_LANES] = h0re_ref[...].reshape(bb // SUBLANES, SUBLANES, S5_LANES)
        hst_ref[:, :, S5_LANES:] = h0im_ref[...].reshape(bb // SUBLANES, SUBLANES, S5_LANES)

    blk = 8
    for gi in range(bb // SUBLANES):
        for c0 in range(0, n_tiles, blk):
            lanes = [slice((c0 + j) * LANES, (c0 + j + 1) * LANES) for j in range(blk)]
            a_re = [lbre_ref[:, s] for s in lanes]
            a_im = [lbim_ref[:, s] for s in lanes]

            def step(i, carry):
                idx = pl.ds(pl.multiple_of((gi * tt + i) * SUBLANES, SUBLANES), SUBLANES)
                new = []
                for j in range(blk):
                    h_re, h_im = carry[j]
                    n_re = a_re[j] * h_re - a_im[j] * h_im + hbuf_ref[c0 + j, idx, :]
                    n_im = a_re[j] * h_im + a_im[j] * h_re + hbuf_ref[n_tiles + c0 + j, idx, :]
                    hbuf_ref[c0 + j, idx, :] = n_re
                    hbuf_ref[n_tiles + c0 + j, idx, :] = n_im
                    new.append((n_re, n_im))
                return tuple(new)

            init = tuple((hst_ref[gi, :, s], hst_ref[gi, :, slice(S5_LANES + s.start, S5_LANES + s.stop)])
                         for s in lanes)
            fin = lax.fori_loop(0, tt, step, init, unroll=4)
            for j, s in enumerate(lanes):
                hst_ref[gi, :, s] = fin[j][0]
                hst_ref[gi, :, slice(S5_LANES + s.start, S5_LANES + s.stop)] = fin[j][1]

    y_halves = []
    for n in range(2):
        h_re = jnp.concatenate([hbuf_ref[n * half_t + c] for c in range(half_t)], axis=1)
        h_im = jnp.concatenate([hbuf_ref[n_tiles + n * half_t + c] for c in range(half_t)], axis=1)
        y_halves.append(jnp.dot(h_re.astype(BF16), wcre_ref[n], preferred_element_type=F32)
                        - jnp.dot(h_im.astype(BF16), wcim_ref[n], preferred_element_type=F32))
    y = jnp.concatenate(y_halves, axis=1) + d_ref[...] * u2
    yb = jnp.dot(permt_ref[...], jax.nn.gelu(y).astype(BF16), preferred_element_type=F32)
    yb_ref[...] = yb.reshape(yb_ref.shape).astype(BF16)
    hre_ref[...] = hst_ref[:, :, :S5_LANES].reshape(bb, S5_LANES)
    him_ref[...] = hst_ref[:, :, S5_LANES:].reshape(bb, S5_LANES)


def _s5(u, h0_re, h0_im, w_b, lb_re, lb_im, wc_re, wc_im, d_row, *, t_len, bb, tt):
    m = u.shape[0]
    bsz = m // t_len
    grid = (bsz // bb, t_len // tt)
    if tt == t_len:
        view = (1, m, S5_CH)
        tok = pl.BlockSpec((1, bb * tt, S5_CH), lambda b, t: (0, b, 0))
    else:
        view = (bsz, t_len, S5_CH)
        tok = pl.BlockSpec((bb, tt, S5_CH), lambda b, t: (b, t, 0))
    full = lambda a: pl.BlockSpec(a.shape, lambda b, t: (0,) * a.ndim, pipeline_mode=pl.Buffered(1))
    st = pl.BlockSpec((bb, S5_LANES), lambda b, t: (b, 0))
    sds = jax.ShapeDtypeStruct
    perm = _s5_perm(bb, tt)
    perm, perm_t = jnp.asarray(perm, BF16), jnp.asarray(perm.T, BF16)
    yb, h_re, h_im = pl.pallas_call(
        functools.partial(_s5_kernel, bb=bb, tt=tt), grid=grid,
        in_specs=[tok, st, st, full(perm), full(perm_t), full(w_b), full(lb_re), full(lb_im), full(wc_re),
                  full(wc_im), full(d_row)],
        out_specs=[tok, st, st],
        out_shape=[sds(view, BF16), sds((bsz, S5_LANES), F32), sds((bsz, S5_LANES), F32)],
        scratch_shapes=[pltpu.VMEM((2 * S5_LANES // LANES, bb * tt, LANES), F32),
                        pltpu.VMEM((bb // SUBLANES, SUBLANES, 2 * S5_LANES), F32)],
        compiler_params=_params("parallel", "arbitrary"),
        name="s5",
    )(u.reshape(view), h0_re, h0_im, perm, perm_t, w_b, lb_re, lb_im, wc_re, wc_im, d_row)
    return yb.reshape(m, S5_CH), h_re, h_im


def _merge_kernel(x_ref, og_ref, yb_ref, wdn_ref, wglu_ref, wgab_ref, wout_ref, g_ref, b_ref, o_ref, *, alpha):
    x = x_ref[...]
    branch_a = jnp.dot(og_ref[...], wdn_ref[...], preferred_element_type=F32)
    glu = jnp.dot(yb_ref[...], wglu_ref[...], preferred_element_type=F32)
    branch_b = glu[:, :D_MODEL] * jax.nn.sigmoid(glu[:, D_MODEL:])
    gab = jnp.dot(x.astype(BF16), wgab_ref[...], preferred_element_type=F32)
    mix_in = jax.nn.sigmoid(gab[:, :D_MODEL]) * branch_a + jax.nn.sigmoid(gab[:, D_MODEL:]) * branch_b
    mix = jnp.dot(mix_in.astype(BF16), wout_ref[...], preferred_element_type=F32)
    o_ref[...] = _layer_norm(alpha * x + mix, g_ref[...], b_ref[...])


def _merge(x, og, yb, wdn, wglu, wgab, wout, ln_g, ln_b, *, alpha):
    m = x.shape[0]
    tm = MERGE_TILE
    row = lambda w: pl.BlockSpec((tm, w), lambda i: (i, 0))
    full = lambda a: pl.BlockSpec(a.shape, lambda i: (0,) * a.ndim, pipeline_mode=pl.Buffered(1))
    return pl.pallas_call(
        functools.partial(_merge_kernel, alpha=alpha), grid=(m // tm,),
        in_specs=[row(D_MODEL), row(V_W), row(S5_CH), full(wdn), full(wglu), full(wgab), full(wout),
                  full(ln_g), full(ln_b)],
        out_specs=row(D_MODEL), out_shape=jax.ShapeDtypeStruct((m, D_MODEL), F32),
        compiler_params=_params("parallel"),
        name="merge",
    )(x, og, yb, wdn, wglu, wgab, wout, ln_g, ln_b)


def _route(x, wr_ref, br_ref):
    x_hi, x_lo = _split2(x)
    hh = jnp.dot(x_hi, wr_ref[...], preferred_element_type=F32)
    logits = (hh[:, :LANES] + hh[:, LANES:]
              + jnp.dot(x_lo, wr_ref[:, :LANES], preferred_element_type=F32)) + br_ref[...]
    lane = lax.broadcasted_iota(jnp.int32, logits.shape, 1).astype(F32)
    first = lambda mask: jnp.min(jnp.where(mask, lane, float(LANES)), -1, keepdims=True)
    lg = jnp.where(lane < MOE_GROUPS, logits, NEG_BIG)
    g_max = jnp.max(lg, -1, keepdims=True)
    g_idx = first(lg == g_max)
    g_w = 1.0 / jnp.sum(jnp.where(lane < MOE_GROUPS, jnp.exp(lg - g_max), 0.0), -1, keepdims=True)
    lo = MOE_GROUPS + MOE_PER_GROUP * g_idx
    le = jnp.where((lane >= lo) & (lane < lo + MOE_PER_GROUP), logits, NEG_BIG)
    m1 = jnp.max(le, -1, keepdims=True)
    i1 = first(le == m1)
    le2 = jnp.where(lane == i1, NEG_BIG, le)
    m2 = jnp.max(le2, -1, keepdims=True)
    i2 = first(le2 == m2)
    e2 = jnp.exp(m2 - m1)
    w1 = g_w / (1.0 + e2)
    w2 = g_w * e2 / (1.0 + e2)
    return jnp.where(lane == i1, w1, 0.0) + jnp.where(lane == i2, w2, 0.0), g_idx


def _moe_kernel(x_ref, p_ref, wr_ref, br_ref, wg_ref, wu_ref, wd_ref, g2_ref, b2_ref, wpg_ref, wple_ref,
                g3_ref, b3_ref, o_ref, xs_ref, combs_ref, acc_ref, *, alpha):
    tm = x_ref.shape[0]
    x = x_ref[...]
    comb, g_idx = _route(x, wr_ref, br_ref)
    lane = lax.broadcasted_iota(jnp.int32, comb.shape, 1).astype(F32)
    onehot = jnp.where(lane == g_idx, 1.0, 0.0)
    ri = lax.broadcasted_iota(jnp.int32, (tm, tm), 0)
    ci = lax.broadcasted_iota(jnp.int32, (tm, tm), 1)
    before = jnp.where(ri > ci, 1.0, 0.0).astype(BF16)
    rank = jnp.dot(before, onehot.astype(BF16), preferred_element_type=F32)
    cnt = jnp.sum(onehot, axis=0, keepdims=True)
    lane_row = lane[:1]
    starts, start_row, run = [], jnp.zeros_like(cnt), jnp.zeros((1, 1), F32)
    for g in range(MOE_GROUPS):
        starts.append(run)
        start_row = start_row + jnp.where(lane_row == g, run, 0.0)
        run = run + cnt[:, g:g + 1]
    dest = jnp.sum(onehot * (rank + start_row), -1, keepdims=True)
    unsort = jnp.where(ci.astype(F32) == dest, 1.0, 0.0)
    sort = unsort.T.astype(BF16)
    unsort = unsort.astype(BF16)
    xs_ref[...] = jnp.dot(sort, x.astype(BF16), preferred_element_type=F32).astype(BF16)
    combs_ref[...] = sum(jnp.dot(sort, piece, preferred_element_type=F32) for piece in _split3(comb))
    acc_ref[...] = jnp.zeros_like(acc_ref)

    for b in range(tm // MOE_BLOCK):
        rows = slice(b * MOE_BLOCK, (b + 1) * MOE_BLOCK)
        for g in range(MOE_GROUPS):
            seg_lo = starts[g][0, 0]
            seg_hi = seg_lo + cnt[0, g]

            @pl.when((seg_lo < (b + 1) * MOE_BLOCK) & (seg_hi > b * MOE_BLOCK))
            def _():
                xb = xs_ref[rows, :]
                cb = combs_ref[rows, :]
                lane_b = lax.broadcasted_iota(jnp.int32, cb.shape, 1)
                hs = []
                for i in range(g * MOE_PER_GROUP, (g + 1) * MOE_PER_GROUP):
                    gate = jnp.dot(xb, wg_ref[i], preferred_element_type=F32)
                    up = jnp.dot(xb, wu_ref[i], preferred_element_type=F32)
                    c_e = jnp.sum(jnp.where(lane_b == i + MOE_GROUPS, cb, 0.0), -1, keepdims=True)
                    hs.append((_silu(gate) * up * c_e).astype(BF16))
                k0 = g * MOE_PER_GROUP * MOE_HIDDEN
                acc_ref[rows, :] += jnp.dot(jnp.concatenate(hs, axis=1), wd_ref[k0:k0 + MOE_PER_GROUP * MOE_HIDDEN, :],
                                            preferred_element_type=F32)

    moe = sum(jnp.dot(unsort, piece, preferred_element_type=F32) for piece in _split3(acc_ref[...]))
    x2 = _layer_norm(alpha * x + moe, g2_ref[...], b2_ref[...])
    gate = jax.nn.sigmoid(jnp.dot(x2.astype(BF16), wpg_ref[...], preferred_element_type=F32))
    ple = gate * jnp.dot(p_ref[...].astype(BF16), wple_ref[...], preferred_element_type=F32)
    o_ref[...] = _layer_norm(alpha * x2 + ple, g3_ref[...], b3_ref[...])


def _moe(x, p, wr, b_r, wg, wu, wd, g2, b2, wpg, wple, g3, b3, *, alpha):
    m = x.shape[0]
    tm = MOE_TILE
    row = lambda w: pl.BlockSpec((tm, w), lambda i: (i, 0))
    full = lambda a: pl.BlockSpec(a.shape, lambda i: (0,) * a.ndim, pipeline_mode=pl.Buffered(1))
    wd = wd.reshape(MOE_EXPERTS * MOE_HIDDEN, D_MODEL)
    return pl.pallas_call(
        functools.partial(_moe_kernel, alpha=alpha), grid=(m // tm,),
        in_specs=[row(D_MODEL), row(PLE_DIM), full(wr), full(b_r), full(wg), full(wu), full(wd),
                  full(g2), full(b2), full(wpg), full(wple), full(g3), full(b3)],
        out_specs=row(D_MODEL), out_shape=jax.ShapeDtypeStruct((m, D_MODEL), F32),
        scratch_shapes=[pltpu.VMEM((tm, D_MODEL), BF16), pltpu.VMEM((tm, LANES), F32), pltpu.VMEM((tm, D_MODEL), F32)],
        compiler_params=_params("parallel"),
        name="moe",
    )(x, p, wr, b_r, wg, wu, wd, g2, b2, wpg, wple, g3, b3)


def _prep_weights(i, w_in, conv_w, dn_a_log, dn_dt_bias, dn_norm_w, w_dn_out,
                  s5_lam_re, s5_lam_im, s5_log_dt, s5_b_re, s5_b_im, s5_c_re, s5_c_im, s5_d, w_glu,
                  w_out, ln1_g, ln1_b, w_rg, b_rg, w_re, b_re, w_gate, w_up, w_down, ln2_g, ln2_b,
                  w_ple, w_ple_gate, ln3_g, ln3_b):
    o_z = CONV_CH
    o_b = o_z + V_W
    o_u = o_b + 2 * DN_HEADS
    o_ga = o_u + S5_CH
    wi = w_in[i]
    row = lambda a: a.reshape(1, -1)
    head_pad = lambda a, off: jnp.zeros((1, LANES), F32).at[0, off:off + DN_HEADS].set(a)
    lb_re, lb_im, w_b = _s5_params(s5_lam_re[i], s5_lam_im[i], s5_log_dt[i], s5_b_re[i], s5_b_im[i])
    wc_re, wc_im = _s5_c_weights(s5_c_re[i], s5_c_im[i])
    w_router = jnp.zeros((D_MODEL, LANES), F32)
    w_router = w_router.at[:, :MOE_GROUPS].set(w_rg[i]).at[:, MOE_GROUPS:MOE_GROUPS + MOE_EXPERTS].set(w_re[i])
    b_router = jnp.zeros((1, LANES), F32)
    b_router = b_router.at[0, :MOE_GROUPS].set(b_rg[i]).at[0, MOE_GROUPS:MOE_GROUPS + MOE_EXPERTS].set(b_re[i])
    wr_hi = w_router.astype(BF16)
    wr_lo = (w_router - wr_hi.astype(F32)).astype(BF16)
    return dict(
        wqkv=wi[:, :CONV_CH].astype(BF16), wz=wi[:, o_z:o_b].astype(BF16),
        wba=jnp.pad(wi[:, o_b:o_u], ((0, 0), (0, LANES - 2 * DN_HEADS))).astype(BF16),
        wu=wi[:, o_u:o_ga].astype(BF16), wgab=wi[:, o_ga:].astype(BF16),
        conv_w=conv_w[i], alog=head_pad(dn_a_log[i], DN_HEADS), dtb=head_pad(dn_dt_bias[i], DN_HEADS),
        norm_w=row(dn_norm_w[i]), wdn=w_dn_out[i].astype(BF16),
        lb_re=lb_re, lb_im=lb_im, w_b=w_b, wc_re=wc_re, wc_im=wc_im, s5_d=row(s5_d[i]),
        wglu=w_glu[i].astype(BF16), wout=w_out[i].astype(BF16), ln1_g=row(ln1_g[i]), ln1_b=row(ln1_b[i]),
        wr=jnp.concatenate([wr_hi, wr_lo], axis=1), b_r=b_router,
        wg=w_gate[i].astype(BF16), wup=w_up[i].astype(BF16), wd=w_down[i].astype(BF16),
        ln2_g=row(ln2_g[i]), ln2_b=row(ln2_b[i]), wpg=w_ple_gate[i].astype(BF16), wple=w_ple[i].astype(BF16),
        ln3_g=row(ln3_g[i]), ln3_b=row(ln3_b[i]),
    )


def _layer(x, p, conv_buf, s_delta, h_re, h_im, w, alpha):
    bsz, t_len, _ = x.shape
    m = bsz * t_len
    assert t_len >= DN_CONV - 1
    if t_len % LONG_TILE == 0:
        bb, tt = 1, LONG_TILE
        s5_bb, s5_tt = SUBLANES, S5_ROWS // SUBLANES
    else:
        assert ROW_TILE % t_len == 0 and S5_ROWS % t_len == 0
        bb, tt = ROW_TILE // t_len, t_len
        s5_bb, s5_tt = S5_ROWS // t_len, t_len
    x2d = x.reshape(m, D_MODEL)
    q, k, v, gz, bg, u, conv_new = _inproj(x2d, conv_buf, w["wqkv"], w["wz"], w["wba"], w["wu"], w["conv_w"],
                                           w["alog"], w["dtb"], t_len=t_len, bb=bb, tt=tt)
    og, s_new = _delta(q, k, v, gz, bg, s_delta, w["norm_w"], t_len=t_len)
    yb, hr_new, hi_new = _s5(u, h_re.reshape(bsz, S5_LANES), h_im.reshape(bsz, S5_LANES), w["w_b"], w["lb_re"],
                             w["lb_im"], w["wc_re"], w["wc_im"], w["s5_d"], t_len=t_len, bb=s5_bb, tt=s5_tt)
    x1 = _merge(x2d, og, yb, w["wdn"], w["wglu"], w["wgab"], w["wout"], w["ln1_g"], w["ln1_b"], alpha=alpha)
    x3 = _moe(x1, p.reshape(m, PLE_DIM), w["wr"], w["b_r"], w["wg"], w["wup"], w["wd"], w["ln2_g"],
              w["ln2_b"], w["wpg"], w["wple"], w["ln3_g"], w["ln3_b"], alpha=alpha)
    return (x3.reshape(bsz, t_len, D_MODEL), s_new, conv_new,
            hr_new.reshape(bsz, S5_GROUPS, S5_STATE), hi_new.reshape(bsz, S5_GROUPS, S5_STATE))


def kernel(x_prompt, x_sample, state_delta, state_conv, state_ssm_re, state_ssm_im, p_prompt, p_sample, w_in, conv_w, dn_a_log, dn_dt_bias, dn_norm_w, w_dn_out, s5_lam_re, s5_lam_im, s5_log_dt, s5_b_re, s5_b_im, s5_c_re, s5_c_im, s5_d, w_glu, w_out, ln1_g, ln1_b, w_rg, b_rg, w_re, b_re, w_gate, w_up, w_down, ln2_g, ln2_b, w_ple, w_ple_gate, ln3_g, ln3_b):
    weights = (w_in, conv_w, dn_a_log, dn_dt_bias, dn_norm_w, w_dn_out, s5_lam_re, s5_lam_im, s5_log_dt, s5_b_re,
               s5_b_im, s5_c_re, s5_c_im, s5_d, w_glu, w_out, ln1_g, ln1_b, w_rg, b_rg, w_re, b_re, w_gate, w_up,
               w_down, ln2_g, ln2_b, w_ple, w_ple_gate, ln3_g, ln3_b)
    depth = w_in.shape[0]
    alpha = (2 * depth) ** 0.25
    bp = x_prompt.shape[0]
    y_p, y_s = x_prompt, x_sample
    outs_p, outs_s = [], []
    for i in range(depth):
        w = _prep_weights(i, *weights)
        y_p, *st_p = _layer(y_p, p_prompt[i], jnp.zeros((bp, DN_CONV - 1, CONV_CH), F32),
                            jnp.zeros((bp, DN_HEADS, DN_DK, DN_DV), F32),
                            jnp.zeros((bp, S5_GROUPS, S5_STATE), F32), jnp.zeros((bp, S5_GROUPS, S5_STATE), F32),
                            w, alpha)
        y_s, *st_s = _layer(y_s, p_sample[i], state_conv[i], state_delta[i], state_ssm_re[i], state_ssm_im[i],
                            w, alpha)
        outs_p.append(st_p)
        outs_s.append(st_s)
    stack = lambda outs, j: jnp.stack([o[j] for o in outs])
    return (y_p, y_s, stack(outs_p, 0), stack(outs_p, 1), stack(outs_p, 2), stack(outs_p, 3),
            stack(outs_s, 0), stack(outs_s, 1), stack(outs_s, 2), stack(outs_s, 3))
```
